```python
import jax, jax.numpy as jnp
from jax import lax
import numpy as np

D_MODEL = 1024
BATCH = 4
SEQ = 8192
DEPTH = 2
DEC_BATCH = 128
DEC_SEQ = 1
PAST_LEN = 16384
PAGE_SIZE = 128

SB_HEADS = 16
SB_KV_HEADS = 4
SB_HEAD_DIM = 64
SB_GROUP = SB_HEADS // SB_KV_HEADS
MLA_HEADS = 16
MLA_NOPE = 128
MLA_ROPE = 64
MLA_V = 128
MLA_Q_LORA = 768
MLA_KV_LORA = 256
MLA_SCALE = (MLA_NOPE + MLA_ROPE) ** -0.5
ROPE_THETA = 10000.0
D_FF = 2816
N_EXPERTS = 8
TOP_K = 2
EXPERT_FF = 3584
Q_BLOCK = 128
NORM_EPS = 1e-6

kernel_name = 'stickbreak_mla_hybrid_step'


def rmsnorm(x, g):
    x32 = x.astype(jnp.float32)
    y = x32 * lax.rsqrt(jnp.mean(x32 * x32, axis=-1, keepdims=True) + NORM_EPS)
    return (y * g.astype(jnp.float32)).astype(x.dtype)


def rope(x, pos):
    half = MLA_ROPE // 2
    freqs = ROPE_THETA ** (-jnp.arange(half, dtype=jnp.float32) * 2.0 / MLA_ROPE)
    ang = pos.astype(jnp.float32)[:, None] * freqs[None, :]
    cos = jnp.cos(ang)[None, :, None, :]
    sin = jnp.sin(ang)[None, :, None, :]
    x32 = x.astype(jnp.float32)
    x1, x2 = x32[..., :half], x32[..., half:]
    return jnp.concatenate([x1 * cos - x2 * sin, x1 * sin + x2 * cos], axis=-1).astype(x.dtype)


def over_query_blocks(fn, q_pos, *qs):
    t = q_pos.shape[0]
    qb = Q_BLOCK if t % Q_BLOCK == 0 else t
    nb = t // qb

    def split(a):
        return jnp.moveaxis(a.reshape(a.shape[0], nb, qb, *a.shape[2:]), 1, 0)

    outs = lax.map(lambda args: fn(*args), (q_pos.reshape(nb, qb), *[split(a) for a in qs]))
    outs = jnp.moveaxis(outs, 0, 1)
    return outs.reshape(outs.shape[0], t, *outs.shape[3:])


def stick_breaking_block(q_pos, q, k, v, k_pos):
    z = jnp.einsum('bqhgd,bshd->bhgqs', q, k).astype(jnp.float32) * (SB_HEAD_DIM ** -0.5)
    causal = k_pos[None, :] < q_pos[:, None]
    log_fail = jnp.where(causal, jax.nn.log_sigmoid(-z), 0.0)
    suffix = lax.cumsum(log_fail, axis=4, reverse=True) - log_fail
    a = jnp.where(causal, jnp.exp(jax.nn.log_sigmoid(z) + suffix), 0.0)
    return jnp.einsum('bhgqs,bshd->bqhgd', a.astype(v.dtype), v)


def stick_breaking_mixer(h, q_pos, past_k, past_v, w_qkv, w_o):
    b, t, _ = h.shape
    nq = SB_HEADS * SB_HEAD_DIM
    nkv = SB_KV_HEADS * SB_HEAD_DIM
    qkv = h @ w_qkv
    q = qkv[..., :nq].reshape(b, t, SB_KV_HEADS, SB_GROUP, SB_HEAD_DIM)
    k_new = qkv[..., nq:nq + nkv].reshape(b, t, SB_KV_HEADS, SB_HEAD_DIM)
    v_new = qkv[..., nq + nkv:].reshape(b, t, SB_KV_HEADS, SB_HEAD_DIM)
    k = jnp.concatenate([past_k.astype(k_new.dtype), k_new], axis=1)
    v = jnp.concatenate([past_v.astype(v_new.dtype), v_new], axis=1)
    k_pos = jnp.arange(k.shape[1], dtype=jnp.int32)
    o = over_query_blocks(lambda p, qq: stick_breaking_block(p, qq, k, v, k_pos), q_pos, q)
    return o.reshape(b, t, nq) @ w_o, k_new, v_new


def mla_block(q_pos, q_lat, q_rope, lat, krope, k_pos):
    s = (jnp.einsum('bqhc,bsc->bhqs', q_lat, lat).astype(jnp.float32)
         + jnp.einsum('bqhr,bsr->bhqs', q_rope, krope).astype(jnp.float32)) * MLA_SCALE
    s = jnp.where(k_pos[None, :] <= q_pos[:, None], s, -jnp.inf)
    p = jax.nn.softmax(s, axis=-1)
    return jnp.einsum('bhqs,bsc->bqhc', p.astype(lat.dtype), lat)


def mla_mixer(h, q_pos, past_lat, past_rope, w_dq, q_norm, w_uq, w_dkv, kv_norm, w_uk, w_uv, w_o):
    b, t, _ = h.shape
    c_q = rmsnorm(h @ w_dq, q_norm)
    q = (c_q @ w_uq).reshape(b, t, MLA_HEADS, MLA_NOPE + MLA_ROPE)
    q_nope = q[..., :MLA_NOPE]
    q_rope = rope(q[..., MLA_NOPE:], q_pos)
    kv = h @ w_dkv
    lat_new = rmsnorm(kv[..., :MLA_KV_LORA], kv_norm)
    rope_new = rope(kv[..., None, MLA_KV_LORA:], q_pos)[:, :, 0]
    q_lat = jnp.einsum('bthn,chn->bthc', q_nope, w_uk)
    lat = jnp.concatenate([past_lat.astype(lat_new.dtype), lat_new], axis=1)
    krope = jnp.concatenate([past_rope.astype(rope_new.dtype), rope_new], axis=1)
    k_pos = jnp.arange(lat.shape[1], dtype=jnp.int32)
    o_lat = over_query_blocks(lambda p, ql, qr: mla_block(p, ql, qr, lat, krope, k_pos),
                              q_pos, q_lat, q_rope)
    o = jnp.einsum('bthc,chv->bthv', o_lat, w_uv).reshape(b, t, MLA_HEADS * MLA_V)
    return o @ w_o, lat_new, rope_new


def swiglu(h, w_gate, w_up, w_down):
    return (jax.nn.silu(h @ w_gate) * (h @ w_up)) @ w_down


def moe_swiglu(h, w_router, w_gate, w_up, w_down):
    logits = (h @ w_router).astype(jnp.float32)
    top_val, top_idx = lax.top_k(logits, TOP_K)
    top_w = jax.nn.softmax(top_val, axis=-1)
    gates = jnp.sum(jax.nn.one_hot(top_idx, N_EXPERTS, dtype=jnp.float32) * top_w[..., None], axis=-2)
    out = jnp.zeros_like(h)
    for e in range(N_EXPERTS):
        out = out + gates[..., e:e + 1].astype(h.dtype) * swiglu(h, w_gate[e], w_up[e], w_down[e])
    return out


def trunk(x, q_pos, sb_past_k, sb_past_v, mla_past_lat, mla_past_rope, weights):
    (ln_mix, ln_ffn, ln_final, sb_w_qkv, sb_w_o, mla_w_dq, mla_q_norm, mla_w_uq, mla_w_dkv,
     mla_kv_norm, mla_w_uk, mla_w_uv, mla_w_o, ffn_w_gate, ffn_w_up, ffn_w_down,
     moe_w_router, moe_w_gate, moe_w_up, moe_w_down) = weights
    sb_k = sb_v = lat = krope = None
    for i in range(DEPTH):
        h = rmsnorm(x, ln_mix[i])
        if i % 2 == 0:
            mix, sb_k, sb_v = stick_breaking_mixer(h, q_pos, sb_past_k, sb_past_v, sb_w_qkv, sb_w_o)
        else:
            mix, lat, krope = mla_mixer(h, q_pos, mla_past_lat, mla_past_rope, mla_w_dq, mla_q_norm,
                                        mla_w_uq, mla_w_dkv, mla_kv_norm, mla_w_uk, mla_w_uv, mla_w_o)
        x = x + mix
        h = rmsnorm(x, ln_ffn[i])
        if i % 2 == 0:
            x = x + swiglu(h, ffn_w_gate, ffn_w_up, ffn_w_down)
        else:
            x = x + moe_swiglu(h, moe_w_router, moe_w_gate, moe_w_up, moe_w_down)
    return rmsnorm(x, ln_final), sb_k, sb_v, lat, krope


def setup_inputs(seed: int = 0) -> dict:
    key = jax.random.key(seed)
    ks = jax.random.split(key, 32)
    n_pages = PAST_LEN // PAGE_SIZE
    n_pool = (DEC_BATCH * n_pages * 5) // 4
    f32 = jnp.float32

    def w(k, shape, fan_in):
        return jax.random.normal(k, shape, f32) * (fan_in ** -0.5)

    def gain(k, shape):
        return 1.0 + 0.02 * jax.random.normal(k, shape, f32)

    page_table = jax.random.permutation(ks[0], n_pool)[:DEC_BATCH * n_pages]
    page_table = page_table.reshape(DEC_BATCH, n_pages).astype(jnp.int32)
    sb_out = SB_HEADS * SB_HEAD_DIM + 2 * SB_KV_HEADS * SB_HEAD_DIM
    return {
        'x_prompt': jax.random.normal(ks[1], (BATCH, SEQ, D_MODEL), f32),
        'x_sample': jax.random.normal(ks[2], (DEC_BATCH, DEC_SEQ, D_MODEL), f32),
        'cache_sb_k': jax.random.normal(ks[3], (n_pool, PAGE_SIZE, SB_KV_HEADS, SB_HEAD_DIM), f32),
        'cache_sb_v': jax.random.normal(ks[4], (n_pool, PAGE_SIZE, SB_KV_HEADS, SB_HEAD_DIM), f32),
        'cache_mla_latent': jax.random.normal(ks[5], (n_pool, PAGE_SIZE, MLA_KV_LORA), f32),
        'cache_mla_krope': jax.random.normal(ks[6], (n_pool, PAGE_SIZE, MLA_ROPE), f32),
        'page_table': page_table,
        'ln_mix': gain(ks[7], (DEPTH, D_MODEL)),
        'ln_ffn': gain(ks[8], (DEPTH, D_MODEL)),
        'ln_final': gain(ks[9], (D_MODEL,)),
        'sb_w_qkv': w(ks[10], (D_MODEL, sb_out), D_MODEL),
        'sb_w_o': w(ks[11], (SB_HEADS * SB_HEAD_DIM, D_MODEL), SB_HEADS * SB_HEAD_DIM),
        'mla_w_dq': w(ks[12], (D_MODEL, MLA_Q_LORA), D_MODEL),
        'mla_q_norm': gain(ks[13], (MLA_Q_LORA,)),
        'mla_w_uq': w(ks[14], (MLA_Q_LORA, MLA_HEADS * (MLA_NOPE + MLA_ROPE)), MLA_Q_LORA),
        'mla_w_dkv': w(ks[15], (D_MODEL, MLA_KV_LORA + MLA_ROPE), D_MODEL),
        'mla_kv_norm': gain(ks[16], (MLA_KV_LORA,)),
        'mla_w_uk': w(ks[17], (MLA_KV_LORA, MLA_HEADS, MLA_NOPE), MLA_KV_LORA),
        'mla_w_uv': w(ks[18], (MLA_KV_LORA, MLA_HEADS, MLA_V), MLA_KV_LORA),
        'mla_w_o': w(ks[19], (MLA_HEADS * MLA_V, D_MODEL), MLA_HEADS * MLA_V),
        'ffn_w_gate': w(ks[20], (D_MODEL, D_FF), D_MODEL),
        'ffn_w_up': w(ks[21], (D_MODEL, D_FF), D_MODEL),
        'ffn_w_down': w(ks[22], (D_FF, D_MODEL), D_FF),
        'moe_w_router': w(ks[23], (D_MODEL, N_EXPERTS), D_MODEL),
        'moe_w_gate': w(ks[24], (N_EXPERTS, D_MODEL, EXPERT_FF), D_MODEL),
        'moe_w_up': w(ks[25], (N_EXPERTS, D_MODEL, EXPERT_FF), D_MODEL),
        'moe_w_down': w(ks[26], (N_EXPERTS, EXPERT_FF, D_MODEL), EXPERT_FF),
    }


def reference(x_prompt, x_sample, cache_sb_k, cache_sb_v, cache_mla_latent, cache_mla_krope, page_table,
              ln_mix, ln_ffn, ln_final, sb_w_qkv, sb_w_o, mla_w_dq, mla_q_norm, mla_w_uq, mla_w_dkv,
              mla_kv_norm, mla_w_uk, mla_w_uv, mla_w_o, ffn_w_gate, ffn_w_up, ffn_w_down,
              moe_w_router, moe_w_gate, moe_w_up, moe_w_down):
    weights = (ln_mix, ln_ffn, ln_final, sb_w_qkv, sb_w_o, mla_w_dq, mla_q_norm, mla_w_uq, mla_w_dkv,
               mla_kv_norm, mla_w_uk, mla_w_uv, mla_w_o, ffn_w_gate, ffn_w_up, ffn_w_down,
               moe_w_router, moe_w_gate, moe_w_up, moe_w_down)

    def gather_pages(cache):
        g = cache[page_table]
        return g.reshape(g.shape[0], g.shape[1] * g.shape[2], *g.shape[3:])

    bp = x_prompt.shape[0]
    pos_prompt = jnp.arange(x_prompt.shape[1], dtype=jnp.int32)
    empty_k = jnp.zeros((bp, 0, SB_KV_HEADS, SB_HEAD_DIM), cache_sb_k.dtype)
    empty_lat = jnp.zeros((bp, 0, MLA_KV_LORA), cache_mla_latent.dtype)
    empty_rope = jnp.zeros((bp, 0, MLA_ROPE), cache_mla_krope.dtype)
    y_prompt, sb_k_p, sb_v_p, lat_p, rope_p = trunk(
        x_prompt, pos_prompt, empty_k, empty_k, empty_lat, empty_rope, weights)

    past_len = page_table.shape[1] * cache_sb_k.shape[1]
    pos_sample = past_len + jnp.arange(x_sample.shape[1], dtype=jnp.int32)
    y_sample, sb_k_s, sb_v_s, lat_s, rope_s = trunk(
        x_sample, pos_sample, gather_pages(cache_sb_k), gather_pages(cache_sb_v),
        gather_pages(cache_mla_latent), gather_pages(cache_mla_krope), weights)

    return (y_prompt, y_sample, sb_k_p, sb_v_p, lat_p, rope_p, sb_k_s, sb_v_s, lat_s, rope_s)
```

```python
import functools

import jax
import jax.numpy as jnp
from jax import lax
from jax.experimental import pallas as pl
from jax.experimental.pallas import tpu as pltpu

BF16 = jnp.bfloat16
F32 = jnp.float32

NORM_EPS = 1e-6
ROPE_THETA = 10000.0
SB_HEADS = 16
SB_KV_HEADS = 4
SB_HEAD_DIM = 64
SB_GROUP = SB_HEADS // SB_KV_HEADS
MLA_HEADS = 16
MLA_NOPE = 128
MLA_ROPE = 64
MLA_V = 128
MLA_SCALE = (MLA_NOPE + MLA_ROPE) ** -0.5
N_EXPERTS = 8
LANES = 128

EXP_UNDERFLOW = -104.0

VMEM_LIMIT = 48 * 1024 * 1024
MLA_TQ = 512


def _params(*semantics):
    return pltpu.CompilerParams(dimension_semantics=semantics, vmem_limit_bytes=VMEM_LIMIT)


def _row_tile(m):
    for t in (512, 256, 128):
        if m % t == 0:
            return t
    return m


def _dot(a, b):
    return jnp.dot(a, b, preferred_element_type=F32)


def _dot_nt(a, b):
    return lax.dot_general(a, b, (((1,), (1,)), ((), ())), preferred_element_type=F32)


def _rms(x, g):
    return x * lax.rsqrt(jnp.mean(x * x, axis=-1, keepdims=True) + NORM_EPS) * g


def _norm_kernel(x_ref, g_ref, o_ref):
    o_ref[...] = _rms(x_ref[...], g_ref[...]).astype(o_ref.dtype)


def _norm_call(x, g, out_dtype):
    m, d = x.shape
    tm = _row_tile(m)
    return pl.pallas_call(
        _norm_kernel,
        grid=(m // tm,),
        in_specs=[pl.BlockSpec((tm, d), lambda i: (i, 0)), pl.BlockSpec((1, d), lambda i: (0, 0))],
        out_specs=pl.BlockSpec((tm, d), lambda i: (i, 0)),
        out_shape=jax.ShapeDtypeStruct((m, d), out_dtype),
        compiler_params=_params("parallel"),
        name="rmsnorm",
    )(x, g.reshape(1, d))


def _qkv_kernel(h_ref, w_ref, q_ref, k_ref, v_ref, kb_ref, vb_ref, *, nq, nkv):
    r = _dot(h_ref[...], w_ref[...])
    q_ref[...] = (r[:, :nq] * (SB_HEAD_DIM ** -0.5)).astype(BF16)
    k = r[:, nq:nq + nkv]
    v = r[:, nq + nkv:]
    k_ref[...] = k
    v_ref[...] = v
    kb_ref[...] = k.astype(BF16)
    vb_ref[...] = v.astype(BF16)


def _qkv_call(h, w):
    m, d = h.shape
    nq = SB_HEADS * SB_HEAD_DIM
    nkv = SB_KV_HEADS * SB_HEAD_DIM
    tm = _row_tile(m)
    row = lambda n: pl.BlockSpec((tm, n), lambda i: (i, 0))
    return pl.pallas_call(
        functools.partial(_qkv_kernel, nq=nq, nkv=nkv),
        grid=(m // tm,),
        in_specs=[row(d), pl.BlockSpec(w.shape, lambda i: (0, 0))],
        out_specs=[row(nq), row(nkv), row(nkv), row(nkv), row(nkv)],
        out_shape=[jax.ShapeDtypeStruct((m, nq), BF16), jax.ShapeDtypeStruct((m, nkv), F32),
                   jax.ShapeDtypeStruct((m, nkv), F32), jax.ShapeDtypeStruct((m, nkv), BF16),
                   jax.ShapeDtypeStruct((m, nkv), BF16)],
        compiler_params=_params("parallel"),
        name="sb_qkv",
    )(h, w)


def _proj_res_norm_kernel(a_ref, w_ref, x_ref, g_ref, xo_ref, ho_ref):
    x = x_ref[...] + _dot(a_ref[...], w_ref[...])
    xo_ref[...] = x
    ho_ref[...] = _rms(x, g_ref[...]).astype(BF16)


def _proj_res_norm_call(a, w, x, g):
    m, k = a.shape
    d = x.shape[1]
    tm = _row_tile(m)
    return pl.pallas_call(
        _proj_res_norm_kernel,
        grid=(m // tm,),
        in_specs=[pl.BlockSpec((tm, k), lambda i: (i, 0)), pl.BlockSpec((k, d), lambda i: (0, 0)),
                  pl.BlockSpec((tm, d), lambda i: (i, 0)), pl.BlockSpec((1, d), lambda i: (0, 0))],
        out_specs=[pl.BlockSpec((tm, d), lambda i: (i, 0)), pl.BlockSpec((tm, d), lambda i: (i, 0))],
        out_shape=[jax.ShapeDtypeStruct((m, d), F32), jax.ShapeDtypeStruct((m, d), BF16)],
        compiler_params=_params("parallel"),
        name="proj_res_norm",
    )(a, w, x, g.reshape(1, d))


def _softplus(s):
    return jnp.maximum(s, 0.0) + jnp.log1p(jnp.exp(-jnp.abs(s)))


def _split_dot(x, u):
    hi = x.astype(BF16)
    lo = (x - hi.astype(F32)).astype(BF16)
    return _dot(hi, u) + _dot(lo, u)


def _strict_lower_ones(n):
    r = lax.broadcasted_iota(jnp.int32, (n, n), 0)
    c = lax.broadcasted_iota(jnp.int32, (n, n), 1)
    return (r > c).astype(BF16)


def _sb_prompt_kernel(q_ref, k_ref, v_ref, o_ref, acc_ref, c_ref, *, tq, tk):
    i = pl.program_id(1)
    u = _strict_lower_ones(tk)
    rows = SB_GROUP * tq
    qpos = i * tq + lax.broadcasted_iota(jnp.int32, (rows, tk), 0) % tq
    kcol = lax.broadcasted_iota(jnp.int32, (rows, tk), 1)
    hd = SB_HEAD_DIM

    for kvh in range(SB_KV_HEADS):
        qg = jnp.concatenate(
            [q_ref[:, (kvh * SB_GROUP + g) * hd:(kvh * SB_GROUP + g + 1) * hd] for g in range(SB_GROUP)], axis=0)
        acc_ref[...] = jnp.zeros_like(acc_ref)
        c_ref[...] = jnp.zeros_like(c_ref)

        def block(j, masked):
            start = pl.multiple_of(j * tk, tk)
            kb = k_ref[pl.ds(start, tk), kvh * hd:(kvh + 1) * hd]
            vb = v_ref[pl.ds(start, tk), kvh * hd:(kvh + 1) * hd]
            s = _dot_nt(qg, kb)
            lsn = -_softplus(s)
            if masked:
                valid = (start + kcol) < qpos
                lsn = jnp.where(valid, lsn, 0.0)
            c = c_ref[...]
            suffix = c + _split_dot(lsn, u)
            a = jnp.exp(lsn + s + suffix)
            if masked:
                a = jnp.where(valid, a, 0.0)
            acc_ref[...] += _dot(a.astype(BF16), vb)
            c_new = c + jnp.sum(lsn, axis=1, keepdims=True)
            c_ref[...] = c_new
            return jnp.max(c_new)

        j_diag = ((i + 1) * tq - 1) // tk
        j_first_full = (i * tq) // tk - 1
        cmax = jnp.float32(0.0)
        for d in range((tq + tk - 1) // tk):
            j = j_diag - d

            def masked_step(j=j):
                return block(j, True)

            cmax = lax.cond(j > j_first_full, masked_step, lambda: cmax)

        def cond(state):
            j, cm = state
            return jnp.logical_and(j >= 0, cm > EXP_UNDERFLOW)

        def body(state):
            j, _ = state
            return j - 1, block(j, False)

        lax.while_loop(cond, body, (j_first_full, cmax))

        acc = acc_ref[...]
        for g in range(SB_GROUP):
            h = kvh * SB_GROUP + g
            o_ref[:, h * hd:(h + 1) * hd] = acc[g * tq:(g + 1) * tq, :].astype(o_ref.dtype)


def _sb_prompt_call(q, kb, vb, batch, seq, tq=128, tk=128):
    nq = q.shape[1]
    nkv = kb.shape[1]
    n_qt = seq // tq
    return pl.pallas_call(
        functools.partial(_sb_prompt_kernel, tq=tq, tk=tk),
        grid=(batch, n_qt),
        in_specs=[pl.BlockSpec((tq, nq), lambda b, i: (b * n_qt + i, 0)),
                  pl.BlockSpec((seq, nkv), lambda b, i: (b, 0)),
                  pl.BlockSpec((seq, nkv), lambda b, i: (b, 0))],
        out_specs=pl.BlockSpec((tq, nq), lambda b, i: (b * n_qt + i, 0)),
        out_shape=jax.ShapeDtypeStruct(q.shape, BF16),
        scratch_shapes=[pltpu.VMEM((SB_GROUP * tq, SB_HEAD_DIM), F32), pltpu.VMEM((SB_GROUP * tq, 1), F32)],
        compiler_params=_params("parallel", "arbitrary"),
        name="sb_attn_prompt",
    )(q, kb, vb)


def _sb_decode_kernel(pt_ref, q_ref, *refs, pages, page):
    k_refs = refs[:pages]
    v_refs = refs[pages:2 * pages]
    o_ref, acc_ref, c_ref = refs[2 * pages:]
    s_idx = pl.program_id(1)
    hd = SB_HEAD_DIM

    @pl.when(s_idx == 0)
    def _():
        acc_ref[...] = jnp.zeros_like(acc_ref)
        c_ref[...] = jnp.zeros_like(c_ref)

    @pl.when(jnp.max(c_ref[...]) > EXP_UNDERFLOW)
    def _():
        q = q_ref[0]
        u = _strict_lower_ones(page)
        for t in reversed(range(pages)):
            kt = k_refs[t][0].astype(BF16)
            vt = v_refs[t][0].astype(BF16)
            s = _dot(q, kt)
            lsn = -_softplus(s)
            c = c_ref[...]
            suffix = c + _split_dot(lsn, u)
            a = jnp.exp(lsn + s + suffix)
            acc_ref[...] += _dot_nt(a.astype(BF16), vt)
            c_ref[...] = c + jnp.sum(lsn, axis=1, keepdims=True)

    @pl.when(s_idx == pl.num_programs(1) - 1)
    def _():
        acc = acc_ref[...]
        head_kv = lax.broadcasted_iota(jnp.int32, acc.shape, 0) // SB_GROUP
        lane_kv = lax.broadcasted_iota(jnp.int32, acc.shape, 1) // hd
        acc = jnp.where(head_kv == lane_kv, acc, 0.0)
        out = acc[:, 0:hd]
        for g in range(1, SB_KV_HEADS):
            out = out + acc[:, g * hd:(g + 1) * hd]
        o_ref[0] = out.astype(o_ref.dtype)


def _sb_decode_call(q_bd, cache_k, cache_v, page_table, pages=8):
    b, heads, width = q_bd.shape
    n_pages = page_table.shape[1]
    page = cache_k.shape[2]
    steps = n_pages // pages

    def page_spec(t):
        return pl.BlockSpec((1, width, page), lambda bi, s, pt: (pt[bi, n_pages - (s + 1) * pages + t], 0, 0))

    grid_spec = pltpu.PrefetchScalarGridSpec(
        num_scalar_prefetch=1,
        grid=(b, steps),
        in_specs=[pl.BlockSpec((1, heads, width), lambda bi, s, pt: (bi, 0, 0))]
        + [page_spec(t) for t in range(pages)] + [page_spec(t) for t in range(pages)],
        out_specs=pl.BlockSpec((1, heads, SB_HEAD_DIM), lambda bi, s, pt: (bi, 0, 0)),
        scratch_shapes=[pltpu.VMEM((heads, width), F32), pltpu.VMEM((heads, 1), F32)],
    )
    return pl.pallas_call(
        functools.partial(_sb_decode_kernel, pages=pages, page=page),
        grid_spec=grid_spec,
        out_shape=jax.ShapeDtypeStruct((b, heads, SB_HEAD_DIM), BF16),
        compiler_params=_params("parallel", "arbitrary"),
        name="sb_attn_decode",
    )(page_table, q_bd, *([cache_k] * pages), *([cache_v] * pages))


def _mla_down_kernel(h_ref, w_ref, qn_ref, kvn_ref, cos_ref, sin_ref,
                     cq_ref, lat_ref, latb_ref, rope_ref, ropeb_ref, *, n_q, n_lat, n_rope):
    r = _dot(h_ref[...], w_ref[...])
    cq_ref[...] = _rms(r[:, :n_q], qn_ref[...]).astype(BF16)
    lat = _rms(r[:, n_q:n_q + n_lat], kvn_ref[...])
    lat_ref[...] = lat
    latb_ref[...] = lat.astype(BF16)
    o = n_q + n_lat
    rope = r[:, o:o + n_rope] * cos_ref[:, :n_rope] + r[:, o + n_rope:o + 2 * n_rope] * sin_ref[:, :n_rope]
    rope_ref[...] = rope
    ropeb_ref[...] = rope.astype(BF16)


def _mla_down_call(h, w, q_norm, kv_norm, cos, sin, n_q, n_lat, n_rope):
    m, d = h.shape
    tm = _row_tile(m)
    n_pos_tiles = cos.shape[0] // tm
    row = lambda n: pl.BlockSpec((tm, n), lambda i: (i, 0))
    const = lambda a: pl.BlockSpec(a.shape, lambda i: (0, 0))
    pos = pl.BlockSpec((tm, cos.shape[1]), lambda i: (i % n_pos_tiles, 0))
    qn = q_norm.reshape(1, n_q)
    kvn = kv_norm.reshape(1, n_lat)
    return pl.pallas_call(
        functools.partial(_mla_down_kernel, n_q=n_q, n_lat=n_lat, n_rope=n_rope),
        grid=(m // tm,),
        in_specs=[row(d), const(w), const(qn), const(kvn), pos, pos],
        out_specs=[row(n_q), row(n_lat), row(n_lat), row(n_rope), row(n_rope)],
        out_shape=[jax.ShapeDtypeStruct((m, n_q), BF16), jax.ShapeDtypeStruct((m, n_lat), F32),
                   jax.ShapeDtypeStruct((m, n_lat), BF16), jax.ShapeDtypeStruct((m, n_rope), F32),
                   jax.ShapeDtypeStruct((m, n_rope), BF16)],
        compiler_params=_params("parallel"),
        name="mla_down",
    )(h, w, qn, kvn, cos, sin)


def _mla_uq_kernel(c_ref, w_ref, cos_ref, sin_ref, qn_ref, qr_ref, *, n_nope, n_rope, scale, head_major):
    c = c_ref[...]
    qn_ref[...] = (_dot(c, w_ref[:, :n_nope]) * scale).astype(BF16)
    heads = n_rope // MLA_ROPE
    reps = n_rope // cos_ref.shape[1]
    cos = jnp.concatenate([cos_ref[...]] * reps, axis=1)
    sin = jnp.concatenate([sin_ref[...]] * reps, axis=1)
    r = _dot(c, w_ref[:, n_nope:n_nope + n_rope])
    rot = _dot(c, w_ref[:, n_nope + n_rope:])
    qr = ((r * cos + rot * sin) * scale).astype(BF16)
    if head_major:
        for hh in range(heads):
            qr_ref[hh] = qr[:, hh * MLA_ROPE:(hh + 1) * MLA_ROPE]
    else:
        qr_ref[...] = qr


def _mla_uq_call(c_q, w, cos, sin, scale, head_major):
    m, k = c_q.shape
    n_nope = MLA_HEADS * MLA_NOPE
    n_rope = MLA_HEADS * MLA_ROPE
    tm = min(_row_tile(m), 256)
    n_pos_tiles = cos.shape[0] // tm
    pos = pl.BlockSpec((tm, cos.shape[1]), lambda i: (i % n_pos_tiles, 0))
    if head_major:
        qr_spec = pl.BlockSpec((MLA_HEADS, tm, MLA_ROPE), lambda i: (0, i, 0))
        qr_shape = jax.ShapeDtypeStruct((MLA_HEADS, m, MLA_ROPE), BF16)
    else:
        qr_spec = pl.BlockSpec((tm, n_rope), lambda i: (i, 0))
        qr_shape = jax.ShapeDtypeStruct((m, n_rope), BF16)
    return pl.pallas_call(
        functools.partial(_mla_uq_kernel, n_nope=n_nope, n_rope=n_rope, scale=scale, head_major=head_major),
        grid=(m // tm,),
        in_specs=[pl.BlockSpec((tm, k), lambda i: (i, 0)), pl.BlockSpec(w.shape, lambda i: (0, 0)), pos, pos],
        out_specs=[pl.BlockSpec((tm, n_nope), lambda i: (i, 0)), qr_spec],
        out_shape=[jax.ShapeDtypeStruct((m, n_nope), BF16), qr_shape],
        compiler_params=_params("parallel"),
        name="mla_uq",
    )(c_q, w, cos, sin)


def _mm_kernel(x_ref, w_ref, o_ref):
    o_ref[...] = _dot(x_ref[...], w_ref[...]).astype(o_ref.dtype)


def _mm_call(x, w, out_dtype, tn):
    m, k = x.shape
    n = w.shape[1]
    tm = _row_tile(m)
    return pl.pallas_call(
        _mm_kernel,
        grid=(n // tn, m // tm),
        in_specs=[pl.BlockSpec((tm, k), lambda j, i: (i, 0)), pl.BlockSpec((k, tn), lambda j, i: (0, j))],
        out_specs=pl.BlockSpec((tm, tn), lambda j, i: (i, j)),
        out_shape=jax.ShapeDtypeStruct((m, n), out_dtype),
        compiler_params=_params("parallel", "parallel"),
        name="matmul",
    )(x, w)


def _head_mm_kernel(x_ref, w_ref, o_ref, *, scale):
    o_ref[...] = (_dot(x_ref[...], w_ref[0]) * scale).astype(o_ref.dtype)


def _head_mm_call(x, w, out_dtype, scale=1.0):
    m = x.shape[0]
    h, a, b = w.shape
    return pl.pallas_call(
        functools.partial(_head_mm_kernel, scale=scale),
        grid=(h,),
        in_specs=[pl.BlockSpec((m, a), lambda i: (0, i)), pl.BlockSpec((1, a, b), lambda i: (i, 0, 0))],
        out_specs=pl.BlockSpec((m, b), lambda i: (0, i)),
        out_shape=jax.ShapeDtypeStruct((m, h * b), out_dtype),
        compiler_params=_params("parallel"),
        name="head_matmul",
    )(x, w)


def _mla_prompt_kernel(qn_ref, qr_ref, kn_ref, kr_ref, v_ref, o_ref, m_ref, l_ref, acc_ref, *, tq):
    i = pl.program_id(2)
    qn = qn_ref[...]
    qr = qr_ref[0]
    m_ref[...] = jnp.full_like(m_ref, -jnp.inf)
    l_ref[...] = jnp.zeros_like(l_ref)
    acc_ref[...] = jnp.zeros_like(acc_ref)

    def block(j, masked):
        start = pl.multiple_of(j * tq, tq)
        kn = kn_ref[pl.ds(start, tq), :]
        kr = kr_ref[pl.ds(start, tq), :]
        s = _dot_nt(qn, kn) + _dot_nt(qr, kr)
        if masked:
            r = lax.broadcasted_iota(jnp.int32, s.shape, 0)
            c = lax.broadcasted_iota(jnp.int32, s.shape, 1)
            s = jnp.where(c <= r, s, -jnp.inf)
        m_old = m_ref[...]
        m_new = jnp.maximum(m_old, jnp.max(s, axis=1, keepdims=True))
        alpha = jnp.exp(m_old - m_new)
        p = jnp.exp(s - m_new)
        l_ref[...] = alpha * l_ref[...] + jnp.sum(p, axis=1, keepdims=True)
        acc_ref[...] = alpha * acc_ref[...] + _dot(p.astype(BF16), v_ref[pl.ds(start, tq), :])
        m_ref[...] = m_new

    def body(j, carry):
        block(j, False)
        return carry

    lax.fori_loop(0, i, body, 0)
    block(i, True)
    o_ref[...] = (acc_ref[...] / l_ref[...]).astype(o_ref.dtype)


def _mla_prompt_call(qn, qr, kv, kr, batch, seq, tq=512):
    n_qt = seq // tq
    return pl.pallas_call(
        functools.partial(_mla_prompt_kernel, tq=tq),
        grid=(batch, MLA_HEADS, n_qt),
        in_specs=[pl.BlockSpec((tq, MLA_NOPE), lambda b, h, i: (b * n_qt + i, h)),
                  pl.BlockSpec((1, tq, MLA_ROPE), lambda b, h, i: (h, b * n_qt + i, 0)),
                  pl.BlockSpec((seq, MLA_NOPE), lambda b, h, i: (b, h)),
                  pl.BlockSpec((seq, MLA_ROPE), lambda b, h, i: (b, 0)),
                  pl.BlockSpec((seq, MLA_V), lambda b, h, i: (b, MLA_HEADS + h))],
        out_specs=pl.BlockSpec((tq, MLA_V), lambda b, h, i: (b * n_qt + i, h)),
        out_shape=jax.ShapeDtypeStruct((batch * seq, MLA_HEADS * MLA_V), BF16),
        scratch_shapes=[pltpu.VMEM((tq, 1), F32), pltpu.VMEM((tq, 1), F32), pltpu.VMEM((tq, MLA_V), F32)],
        compiler_params=_params("parallel", "parallel", "arbitrary"),
        name="mla_attn_prompt",
    )(qn, qr, kv, kr, kv)


def _mla_decode_kernel(pt_ref, ql_ref, qr_ref, ln_ref, rn_ref, *refs, pages):
    lat_refs = refs[:pages]
    kr_refs = refs[pages:2 * pages]
    o_ref, m_ref, l_ref, acc_ref = refs[2 * pages:]
    s_idx = pl.program_id(1)
    ql = ql_ref[0]
    qr = qr_ref[0]

    @pl.when(s_idx == 0)
    def _():
        ln = ln_ref[0].astype(F32)
        rn = rn_ref[0].astype(F32)
        s_new = (jnp.sum(ql.astype(F32) * ln, axis=1, keepdims=True)
                 + jnp.sum(qr.astype(F32) * rn, axis=1, keepdims=True)) * MLA_SCALE
        m_ref[...] = s_new
        l_ref[...] = jnp.ones_like(l_ref)
        acc_ref[...] = jnp.broadcast_to(ln, acc_ref.shape)

    lat = jnp.concatenate([r[0] for r in lat_refs], axis=0).astype(BF16)
    krt = jnp.concatenate([r[0] for r in kr_refs], axis=1).astype(BF16)
    s = (_dot_nt(ql, lat) + _dot(qr, krt)) * MLA_SCALE
    m_old = m_ref[...]
    m_new = jnp.maximum(m_old, jnp.max(s, axis=1, keepdims=True))
    alpha = jnp.exp(m_old - m_new)
    p = jnp.exp(s - m_new)
    l_ref[...] = alpha * l_ref[...] + jnp.sum(p, axis=1, keepdims=True)
    acc_ref[...] = alpha * acc_ref[...] + _dot(p.astype(BF16), lat)
    m_ref[...] = m_new

    @pl.when(s_idx == pl.num_programs(1) - 1)
    def _():
        o_ref[0] = (acc_ref[...] / l_ref[...]).astype(o_ref.dtype)


def _mla_decode_call(q_lat, q_rope, lat_new, rope_new, cache_lat, cache_rope, page_table, pages=8):
    b, heads, lora = q_lat.shape
    n_rope = q_rope.shape[2]
    n_pages = page_table.shape[1]
    page = cache_lat.shape[1]
    steps = n_pages // pages

    def page_spec(t, shape):
        return pl.BlockSpec((1,) + shape, lambda bi, s, pt: (pt[bi, s * pages + t], 0, 0))

    per_b = lambda shape: pl.BlockSpec((1,) + shape, lambda bi, s, pt: (bi, 0, 0))
    grid_spec = pltpu.PrefetchScalarGridSpec(
        num_scalar_prefetch=1,
        grid=(b, steps),
        in_specs=[per_b((heads, lora)), per_b((heads, n_rope)), per_b((1, lora)), per_b((1, n_rope))]
        + [page_spec(t, (page, lora)) for t in range(pages)] + [page_spec(t, (n_rope, page)) for t in range(pages)],
        out_specs=per_b((heads, lora)),
        scratch_shapes=[pltpu.VMEM((heads, 1), F32), pltpu.VMEM((heads, 1), F32), pltpu.VMEM((heads, lora), F32)],
    )
    return pl.pallas_call(
        functools.partial(_mla_decode_kernel, pages=pages),
        grid_spec=grid_spec,
        out_shape=jax.ShapeDtypeStruct((b, heads, lora), BF16),
        compiler_params=_params("parallel", "arbitrary"),
        name="mla_attn_decode",
    )(page_table, q_lat, q_rope, lat_new.reshape(b, 1, lora), rope_new.reshape(b, 1, n_rope),
      *([cache_lat] * pages), *([cache_rope] * pages))


def _ffn_kernel(h_ref, g_ref, wg_ref, wu_ref, wd_ref, x_ref, o_ref, y_ref, out_ref, *, n_e, n_f):
    e = pl.program_id(1)
    f = pl.program_id(2)

    @pl.when(jnp.logical_and(e == 0, f == 0))
    def _():
        out_ref[...] = jnp.zeros_like(out_ref)

    @pl.when(f == 0)
    def _():
        y_ref[...] = jnp.zeros_like(y_ref)

    h = h_ref[...]
    g = _dot(h, wg_ref[0])
    u = _dot(h, wu_ref[0])
    act = (g * jax.nn.sigmoid(g)) * u
    y_ref[...] += _dot(act.astype(BF16), wd_ref[0])

    @pl.when(f == n_f - 1)
    def _():
        gates = g_ref[...]
        lane = lax.broadcasted_iota(jnp.int32, gates.shape, 1)
        gate = jnp.sum(jnp.where(lane == e, gates, 0.0), axis=1, keepdims=True)
        out_ref[...] += gate * y_ref[...]

    @pl.when(jnp.logical_and(e == n_e - 1, f == n_f - 1))
    def _():
        o_ref[...] = x_ref[...] + out_ref[...]


def _ffn_call(h, gates, wg, wu, wd, x, tf):
    m, d = h.shape
    n_e, _, ff = wg.shape
    n_f = ff // tf
    tm = _row_tile(m)
    return pl.pallas_call(
        functools.partial(_ffn_kernel, n_e=n_e, n_f=n_f),
        grid=(m // tm, n_e, n_f),
        in_specs=[pl.BlockSpec((tm, d), lambda i, e, f: (i, 0)),
                  pl.BlockSpec((tm, LANES), lambda i, e, f: (i, 0)),
                  pl.BlockSpec((1, d, tf), lambda i, e, f: (e, 0, f)),
                  pl.BlockSpec((1, d, tf), lambda i, e, f: (e, 0, f)),
                  pl.BlockSpec((1, tf, d), lambda i, e, f: (e, f, 0)),
                  pl.BlockSpec((tm, d), lambda i, e, f: (i, 0))],
        out_specs=pl.BlockSpec((tm, d), lambda i, e, f: (i, 0)),
        out_shape=jax.ShapeDtypeStruct((m, d), F32),
        scratch_shapes=[pltpu.VMEM((tm, d), F32), pltpu.VMEM((tm, d), F32)],
        compiler_params=_params("parallel", "arbitrary", "arbitrary"),
        name="ffn",
    )(h, gates, wg, wu, wd, x)


def _router_kernel(h_ref, w_ref, g_ref):
    logits = _dot(h_ref[...], w_ref[...])
    lane = lax.broadcasted_iota(jnp.int32, logits.shape, 1)
    logits = jnp.where(lane < N_EXPERTS, logits, -jnp.inf)
    v1 = jnp.max(logits, axis=1, keepdims=True)
    i1 = jnp.min(jnp.where(logits == v1, lane, LANES), axis=1, keepdims=True)
    rest = jnp.where(lane == i1, -jnp.inf, logits)
    v2 = jnp.max(rest, axis=1, keepdims=True)
    i2 = jnp.min(jnp.where(rest == v2, lane, LANES), axis=1, keepdims=True)
    e2 = jnp.exp(v2 - v1)
    w1 = 1.0 / (1.0 + e2)
    w2 = e2 / (1.0 + e2)
    g_ref[...] = jnp.where(lane == i1, w1, 0.0) + jnp.where(lane == i2, w2, 0.0)


def _router_call(h, w_pad):
    m, d = h.shape
    tm = _row_tile(m)
    return pl.pallas_call(
        _router_kernel,
        grid=(m // tm,),
        in_specs=[pl.BlockSpec((tm, d), lambda i: (i, 0)), pl.BlockSpec((d, LANES), lambda i: (0, 0))],
        out_specs=pl.BlockSpec((tm, LANES), lambda i: (i, 0)),
        out_shape=jax.ShapeDtypeStruct((m, LANES), F32),
        compiler_params=_params("parallel"),
        name="router",
    )(h, w_pad)


def _rope_tables(pos, width):
    half = MLA_ROPE // 2
    freqs = ROPE_THETA ** (-jnp.arange(half, dtype=F32) * 2.0 / MLA_ROPE)
    ang = pos.astype(F32)[:, None] * freqs[None, :]
    reps = width // half
    return jnp.tile(jnp.cos(ang), (1, reps)), jnp.tile(jnp.sin(ang), (1, reps))


def _rotate_half_columns(w):
    k, n = w.shape
    half = MLA_ROPE // 2
    w4 = w.reshape(k, n // MLA_ROPE, 2, half)
    return jnp.concatenate([-w4[:, :, 1:2], w4[:, :, 0:1]], axis=2).reshape(k, n)


def _prepare_weights(sb_w_qkv, sb_w_o, mla_w_dq, mla_w_uq, mla_w_dkv, mla_w_uk, mla_w_uv, mla_w_o,
                     ffn_w_gate, ffn_w_up, ffn_w_down, moe_w_router, moe_w_gate, moe_w_up, moe_w_down):
    kv_lora = mla_w_uk.shape[0]
    w_rope_k = mla_w_dkv[:, kv_lora:]
    w_down = jnp.concatenate([mla_w_dq, mla_w_dkv, _rotate_half_columns(w_rope_k)], axis=1)
    q_lora = mla_w_uq.shape[0]
    uq = mla_w_uq.reshape(q_lora, MLA_HEADS, MLA_NOPE + MLA_ROPE)
    uq_nope = uq[:, :, :MLA_NOPE].reshape(q_lora, MLA_HEADS * MLA_NOPE)
    uq_rope = uq[:, :, MLA_NOPE:].reshape(q_lora, MLA_HEADS * MLA_ROPE)
    w_uq = jnp.concatenate([uq_nope, uq_rope, _rotate_half_columns(uq_rope)], axis=1)
    w_ukv = jnp.concatenate([mla_w_uk.reshape(kv_lora, -1), mla_w_uv.reshape(kv_lora, -1)], axis=1)
    router = jnp.pad(moe_w_router, ((0, 0), (0, LANES - N_EXPERTS)))
    bf = lambda a: a.astype(BF16)
    return dict(
        qkv=bf(sb_w_qkv), sb_o=bf(sb_w_o), down=bf(w_down), uq=bf(w_uq), ukv=bf(w_ukv),
        uk_heads=bf(jnp.transpose(mla_w_uk, (1, 2, 0))),
        uv_heads=bf(jnp.transpose(mla_w_uv, (1, 0, 2))),
        mla_o=bf(mla_w_o), ffn_gate=bf(ffn_w_gate)[None], ffn_up=bf(ffn_w_up)[None], ffn_down=bf(ffn_w_down)[None],
        router=bf(router), moe_gate=bf(moe_w_gate), moe_up=bf(moe_w_up), moe_down=bf(moe_w_down))


def _trunk(x, pos, w, norms, past):
    ln_mix, ln_ffn, ln_final, mla_q_norm, mla_kv_norm = norms
    batch, seq, d = x.shape
    m = batch * seq
    x2 = x.reshape(m, d)
    prompt = past is None

    h = _norm_call(x2, ln_mix[0], BF16)
    q, k_new, v_new, kb, vb = _qkv_call(h, w["qkv"])
    if prompt:
        o = _sb_prompt_call(q, kb, vb, batch, seq)
    else:
        cache_k, cache_v, cache_lat, cache_rope, page_table = past
        width = SB_KV_HEADS * SB_HEAD_DIM
        q3 = q.reshape(m, SB_HEADS, SB_HEAD_DIM)
        own = (jnp.arange(SB_HEADS)[:, None] // SB_GROUP) == (jnp.arange(width)[None, :] // SB_HEAD_DIM)
        q_bd = jnp.where(own[None], jnp.tile(q3, (1, 1, SB_KV_HEADS)), jnp.zeros((), BF16))
        pages_t = lambda c: jnp.transpose(c, (0, 2, 3, 1)).reshape(c.shape[0], width, c.shape[1])
        o = _sb_decode_call(q_bd, pages_t(cache_k), pages_t(cache_v), page_table)
        o = o.reshape(m, SB_HEADS * SB_HEAD_DIM)
    x2, h = _proj_res_norm_call(o, w["sb_o"], x2, ln_ffn[0])

    ones = jnp.ones((m, LANES), F32)
    x2 = _ffn_call(h, ones, w["ffn_gate"], w["ffn_up"], w["ffn_down"], x2, tf=w["ffn_gate"].shape[2] // 2)

    h = _norm_call(x2, ln_mix[1], BF16)
    cos, sin = _rope_tables(pos if prompt else jnp.tile(pos, batch), LANES)
    n_q = mla_q_norm.shape[0]
    n_lat = mla_kv_norm.shape[0]
    c_q, lat_new, lat_b, rope_new, rope_b = _mla_down_call(
        h, w["down"], mla_q_norm, mla_kv_norm, cos, sin, n_q, n_lat, MLA_ROPE)
    if prompt:
        qn, qr = _mla_uq_call(c_q, w["uq"], cos, sin, MLA_SCALE, head_major=True)
        kv = _mm_call(lat_b, w["ukv"], BF16, tn=w["ukv"].shape[1] // 2)
        o = _mla_prompt_call(qn, qr, kv, rope_b, batch, seq, tq=min(MLA_TQ, seq))
    else:
        qn, qr = _mla_uq_call(c_q, w["uq"], cos, sin, 1.0, head_major=False)
        q_lat = _head_mm_call(qn, w["uk_heads"], BF16).reshape(m, MLA_HEADS, n_lat)
        o_lat = _mla_decode_call(q_lat, qr.reshape(m, MLA_HEADS, MLA_ROPE), lat_b, rope_b,
                                 cache_lat, jnp.transpose(cache_rope, (0, 2, 1)), page_table)
        o = _head_mm_call(o_lat.reshape(m, MLA_HEADS * n_lat), w["uv_heads"], BF16)
    x2, h = _proj_res_norm_call(o, w["mla_o"], x2, ln_ffn[1])

    gates = _router_call(h, w["router"])
    x2 = _ffn_call(h, gates, w["moe_gate"], w["moe_up"], w["moe_down"], x2, tf=w["moe_gate"].shape[2] // 7)

    y = _norm_call(x2, ln_final, F32)
    return (y.reshape(batch, seq, d),
            k_new.reshape(batch, seq, SB_KV_HEADS, SB_HEAD_DIM), v_new.reshape(batch, seq, SB_KV_HEADS, SB_HEAD_DIM),
            lat_new.reshape(batch, seq, n_lat), rope_new.reshape(batch, seq, MLA_ROPE))


def kernel(x_prompt, x_sample, cache_sb_k, cache_sb_v, cache_mla_latent, cache_mla_krope, page_table,
           ln_mix, ln_ffn, ln_final, sb_w_qkv, sb_w_o, mla_w_dq, mla_q_norm, mla_w_uq, mla_w_dkv,
           mla_kv_norm, mla_w_uk, mla_w_uv, mla_w_o, ffn_w_gate, ffn_w_up, ffn_w_down,
           moe_w_router, moe_w_gate, moe_w_up, moe_w_down):
    w = _prepare_weights(sb_w_qkv, sb_w_o, mla_w_dq, mla_w_uq, mla_w_dkv, mla_w_uk, mla_w_uv, mla_w_o,
                         ffn_w_gate, ffn_w_up, ffn_w_down, moe_w_router, moe_w_gate, moe_w_up, moe_w_down)
    norms = (ln_mix, ln_ffn, ln_final, mla_q_norm, mla_kv_norm)

    pos_prompt = jnp.arange(x_prompt.shape[1], dtype=jnp.int32)
    y_p, k_p, v_p, lat_p, rope_p = _trunk(x_prompt, pos_prompt, w, norms, None)

    past_len = page_table.shape[1] * cache_sb_k.shape[1]
    pos_sample = past_len + jnp.arange(x_sample.shape[1], dtype=jnp.int32)
    past = (cache_sb_k, cache_sb_v, cache_mla_latent, cache_mla_krope, page_table)
    y_s, k_s, v_s, lat_s, rope_s = _trunk(x_sample, pos_sample, w, norms, past)

    return (y_p, y_s, k_p, v_p, lat_p, rope_p, k_s, v_s, lat_s, rope_s)
```

```python
import functools

import jax
import jax.numpy as jnp
from jax import lax
from jax.experimental import pallas as pl
from jax.experimental.pallas import tpu as pltpu

BF16 = jnp.bfloat16
F32 = jnp.float32

NORM_EPS = 1e-6
ROPE_THETA = 10000.0
SB_HEADS = 16
SB_KV_HEADS = 4
SB_HEAD_DIM = 64
SB_GROUP = SB_HEADS // SB_KV_HEADS
MLA_HEADS = 16
MLA_NOPE = 128
MLA_ROPE = 64
MLA_V = 128
MLA_SCALE = (MLA_NOPE + MLA_ROPE) ** -0.5
LOG2_E = 1.4426950408889634
N_EXPERTS = 8
LANES = 128

EXP_UNDERFLOW = -104.0

VMEM_LIMIT = 48 * 1024 * 1024
MLA_TQ = 512
SB_DECODE_PAGES = 4
MOE_TM = 512
MOE_TF = 896
FFN_BLOCK_BYTES = 3 * 1024 * 1024


def _params(*semantics):
    return pltpu.CompilerParams(dimension_semantics=semantics, vmem_limit_bytes=VMEM_LIMIT)


def _row_tile(m):
    for t in (512, 256, 128):
        if m % t == 0:
            return t
    return m


def _split(x):
    hi = x.astype(BF16)
    return hi, (x - hi.astype(F32)).astype(BF16)


def _dot_dims(a, b, dims):
    mm = lambda x, y: lax.dot_general(x, y, (dims, ((), ())), preferred_element_type=F32)
    if a.dtype == F32 and b.dtype == F32:
        ah, al = _split(a)
        bh, bl = _split(b)
        return mm(ah, bh) + (mm(ah, bl) + mm(al, bh))
    assert a.dtype == BF16 and b.dtype == BF16, (a.dtype, b.dtype)
    return mm(a, b)


def _dot(a, b):
    return _dot_dims(a, b, ((1,), (0,)))


def _dot_nt(a, b):
    return _dot_dims(a, b, ((1,), (1,)))


def _rms(x, g):
    return x * lax.rsqrt(jnp.mean(x * x, axis=-1, keepdims=True) + NORM_EPS) * g


def _norm_kernel(x_ref, g_ref, o_ref):
    o_ref[...] = _rms(x_ref[...], g_ref[...]).astype(o_ref.dtype)


def _norm_call(x, g, out_dtype):
    m, d = x.shape
    tm = _row_tile(m)
    return pl.pallas_call(
        _norm_kernel,
        grid=(m // tm,),
        in_specs=[pl.BlockSpec((tm, d), lambda i: (i, 0)), pl.BlockSpec((1, d), lambda i: (0, 0))],
        out_specs=pl.BlockSpec((tm, d), lambda i: (i, 0)),
        out_shape=jax.ShapeDtypeStruct((m, d), out_dtype),
        compiler_params=_params("parallel"),
        name="rmsnorm",
    )(x, g.reshape(1, d))


def _qkv_kernel(h_ref, wq_ref, wv_ref, wkvt_ref, q_ref, kt_ref, vt_ref, ktb_ref, vb_ref, *, nkv):
    h = h_ref[...]
    q_ref[...] = (_dot(h, wq_ref[...]) * (SB_HEAD_DIM ** -0.5)).astype(q_ref.dtype)
    vb_ref[...] = _dot(h, wv_ref[...]).astype(vb_ref.dtype)
    kvt = _dot_nt(wkvt_ref[...], h)
    kt_ref[0] = kvt[:nkv]
    vt_ref[0] = kvt[nkv:]
    ktb_ref[0] = kvt[:nkv].astype(ktb_ref.dtype)


def _qkv_call(h, wq, wv, wkv_t, groups):
    m, d = h.shape
    cd = h.dtype
    nq = wq.shape[1]
    nkv = wv.shape[1]
    tpg = m // groups
    tm = _row_tile(tpg)
    tiles = tpg // tm
    row = lambda n: pl.BlockSpec((tm, n), lambda i: (i, 0))
    const = lambda a: pl.BlockSpec(a.shape, lambda i: (0, 0))
    pos_minor = pl.BlockSpec((1, nkv, tm), lambda i: (i // tiles, 0, i % tiles))
    return pl.pallas_call(
        functools.partial(_qkv_kernel, nkv=nkv),
        grid=(m // tm,),
        in_specs=[row(d), const(wq), const(wv), const(wkv_t)],
        out_specs=[row(nq), pos_minor, pos_minor, pos_minor, row(nkv)],
        out_shape=[jax.ShapeDtypeStruct((m, nq), cd), jax.ShapeDtypeStruct((groups, nkv, tpg), F32),
                   jax.ShapeDtypeStruct((groups, nkv, tpg), F32), jax.ShapeDtypeStruct((groups, nkv, tpg), cd),
                   jax.ShapeDtypeStruct((m, nkv), cd)],
        compiler_params=_params("parallel"),
        name="sb_qkv",
    )(h, wq, wv, wkv_t)


def _proj_res_norm_kernel(a_ref, w_ref, x_ref, g_ref, xo_ref, ho_ref):
    x = x_ref[...] + _dot(a_ref[...], w_ref[...])
    xo_ref[...] = x
    ho_ref[...] = _rms(x, g_ref[...]).astype(ho_ref.dtype)


def _proj_res_norm_call(a, w, x, g):
    m, k = a.shape
    d = x.shape[1]
    tm = _row_tile(m)
    return pl.pallas_call(
        _proj_res_norm_kernel,
        grid=(m // tm,),
        in_specs=[pl.BlockSpec((tm, k), lambda i: (i, 0)), pl.BlockSpec((k, d), lambda i: (0, 0)),
                  pl.BlockSpec((tm, d), lambda i: (i, 0)), pl.BlockSpec((1, d), lambda i: (0, 0))],
        out_specs=[pl.BlockSpec((tm, d), lambda i: (i, 0)), pl.BlockSpec((tm, d), lambda i: (i, 0))],
        out_shape=[jax.ShapeDtypeStruct((m, d), F32), jax.ShapeDtypeStruct((m, d), a.dtype)],
        compiler_params=_params("parallel"),
        name="proj_res_norm",
    )(a, w, x, g.reshape(1, d))


def _softplus(s):
    return jnp.maximum(s, 0.0) + jnp.log(1.0 + jnp.exp(-jnp.abs(s)))


def _split_dot(x, u):
    hi, lo = _split(x)
    return _dot(hi, u) + _dot(lo, u)


def _strict_lower_ones(n):
    r = lax.broadcasted_iota(jnp.int32, (n, n), 0)
    c = lax.broadcasted_iota(jnp.int32, (n, n), 1)
    return (r > c).astype(BF16)


def _sb_prompt_kernel(q_ref, kt_ref, v_ref, o_ref, qg_ref, acc_ref, c_ref, *, tq, tk):
    i = pl.program_id(1)
    u = _strict_lower_ones(tk)
    rows = SB_GROUP * tq
    qpos = i * tq + lax.broadcasted_iota(jnp.int32, (rows, tk), 0) % tq
    kcol = lax.broadcasted_iota(jnp.int32, (rows, tk), 1)
    hd = SB_HEAD_DIM

    for kvh in range(SB_KV_HEADS):
        qg_ref[kvh] = jnp.concatenate(
            [q_ref[:, (kvh * SB_GROUP + g) * hd:(kvh * SB_GROUP + g + 1) * hd] for g in range(SB_GROUP)], axis=0)
    acc_ref[...] = jnp.zeros_like(acc_ref)
    c_ref[...] = jnp.zeros_like(c_ref)

    def block(j, masked):
        start = pl.multiple_of(j * tk, tk)
        if masked:
            valid = (start + kcol) < qpos
        cmax = None
        for kvh in range(SB_KV_HEADS):
            kt = kt_ref[0, kvh * hd:(kvh + 1) * hd, pl.ds(start, tk)]
            vb = v_ref[pl.ds(start, tk), kvh * hd:(kvh + 1) * hd]
            s = _dot(qg_ref[kvh], kt)
            lsn = -_softplus(s)
            if masked:
                lsn = jnp.where(valid, lsn, 0.0)
            c = c_ref[kvh]
            suffix = c + _split_dot(lsn, u)
            a = jnp.exp(lsn + s + suffix)
            if masked:
                a = jnp.where(valid, a, 0.0)
            acc_ref[kvh] += _dot(a.astype(BF16), vb)
            c_new = c + jnp.sum(lsn, axis=1, keepdims=True)
            c_ref[kvh] = c_new
            m = jnp.max(c_new)
            cmax = m if cmax is None else jnp.maximum(cmax, m)
        return cmax

    j_diag = ((i + 1) * tq - 1) // tk
    j_first_full = (i * tq) // tk - 1
    cmax = jnp.float32(0.0)
    for d in range((tq + tk - 1) // tk):
        j = j_diag - d

        def masked_step(j=j):
            return block(j, True)

        cmax = lax.cond(j > j_first_full, masked_step, lambda: cmax)

    def cond(state):
        j, cm = state
        return jnp.logical_and(j >= 0, cm > EXP_UNDERFLOW)

    def body(state):
        j, _ = state
        return j - 1, block(j, False)

    lax.while_loop(cond, body, (j_first_full, cmax))

    for kvh in range(SB_KV_HEADS):
        acc = acc_ref[kvh]
        for g in range(SB_GROUP):
            h = kvh * SB_GROUP + g
            o_ref[:, h * hd:(h + 1) * hd] = acc[g * tq:(g + 1) * tq, :].astype(o_ref.dtype)


def _sb_prompt_call(q, ktb, vb, batch, seq, tq=128, tk=LANES):
    assert tk == LANES
    nq = q.shape[1]
    nkv = vb.shape[1]
    n_qt = seq // tq
    rows = SB_GROUP * tq
    return pl.pallas_call(
        functools.partial(_sb_prompt_kernel, tq=tq, tk=tk),
        grid=(batch, n_qt),
        in_specs=[pl.BlockSpec((tq, nq), lambda b, i: (b * n_qt + i, 0)),
                  pl.BlockSpec((1, nkv, seq), lambda b, i: (b, 0, 0)),
                  pl.BlockSpec((seq, nkv), lambda b, i: (b, 0))],
        out_specs=pl.BlockSpec((tq, nq), lambda b, i: (b * n_qt + i, 0)),
        out_shape=jax.ShapeDtypeStruct(q.shape, BF16),
        scratch_shapes=[pltpu.VMEM((SB_KV_HEADS, rows, SB_HEAD_DIM), BF16),
                        pltpu.VMEM((SB_KV_HEADS, rows, SB_HEAD_DIM), F32),
                        pltpu.VMEM((SB_KV_HEADS, rows, LANES), F32)],
        compiler_params=_params("parallel", "arbitrary"),
        name="sb_attn_prompt",
    )(q, ktb, vb)


def _sb_decode_kernel(pt_ref, q_ref, acc_in_ref, c_in_ref, *refs, pages, page):
    k_refs = refs[:pages]
    v_refs = refs[pages:2 * pages]
    o_ref, acc_out_ref, c_out_ref, acc_ref, c_ref = refs[2 * pages:]
    s_idx = pl.program_id(1)
    hd = SB_HEAD_DIM

    @pl.when(s_idx == 0)
    def _():
        acc_ref[...] = acc_in_ref[0]
        c_ref[...] = c_in_ref[0]

    @pl.when(jnp.max(c_ref[...]) > EXP_UNDERFLOW)
    def _():
        q = q_ref[0]
        u = _strict_lower_ones(page)
        for t in reversed(range(pages)):
            kt = k_refs[t][0].astype(q.dtype)
            vt = v_refs[t][0].astype(q.dtype)
            s = _dot(q, kt)
            lsn = -_softplus(s)
            c = c_ref[...]
            suffix = c + _split_dot(lsn, u)
            a = jnp.exp(lsn + s + suffix)
            acc_ref[...] += _dot_nt(a.astype(q.dtype), vt)
            c_ref[...] = c + jnp.sum(lsn, axis=1, keepdims=True)

    @pl.when(s_idx == pl.num_programs(1) - 1)
    def _():
        acc = acc_ref[...]
        acc_out_ref[0] = acc
        c_out_ref[0] = c_ref[...]
        head_kv = lax.broadcasted_iota(jnp.int32, acc.shape, 0) // SB_GROUP
        lane_kv = lax.broadcasted_iota(jnp.int32, acc.shape, 1) // hd
        acc = jnp.where(head_kv == lane_kv, acc, 0.0)
        out = acc[:, 0:hd]
        for g in range(1, SB_KV_HEADS):
            out = out + acc[:, g * hd:(g + 1) * hd]
        o_ref[0] = out.astype(o_ref.dtype)


def _sb_decode_call(q_bd, cache_k, cache_v, page_table, page_lo, page_hi, acc_in, c_in, pages):
    b, heads, width = q_bd.shape
    page = cache_k.shape[2]
    assert page == LANES and (page_hi - page_lo) % pages == 0
    steps = (page_hi - page_lo) // pages

    def page_spec(t):
        return pl.BlockSpec((1, width, page), lambda bi, s, pt: (pt[bi, page_hi - (s + 1) * pages + t], 0, 0))

    per_b = lambda n: pl.BlockSpec((1, heads, n), lambda bi, s, pt: (bi, 0, 0))
    grid_spec = pltpu.PrefetchScalarGridSpec(
        num_scalar_prefetch=1,
        grid=(b, steps),
        in_specs=[per_b(width), per_b(width), per_b(page)]
        + [page_spec(t) for t in range(pages)] + [page_spec(t) for t in range(pages)],
        out_specs=[per_b(SB_HEAD_DIM), per_b(width), per_b(page)],
        scratch_shapes=[pltpu.VMEM((heads, width), F32), pltpu.VMEM((heads, page), F32)],
    )
    return pl.pallas_call(
        functools.partial(_sb_decode_kernel, pages=pages, page=page),
        grid_spec=grid_spec,
        out_shape=[jax.ShapeDtypeStruct((b, heads, SB_HEAD_DIM), q_bd.dtype), jax.ShapeDtypeStruct((b, heads, width), F32),
                   jax.ShapeDtypeStruct((b, heads, page), F32)],
        compiler_params=_params("parallel", "arbitrary"),
        name="sb_attn_decode",
    )(page_table, q_bd, acc_in, c_in, *([cache_k] * pages), *([cache_v] * pages))


def _mla_down_kernel(h_ref, w_ref, wrt_ref, qn_ref, kvn_ref, cost_ref, sint_ref,
                     cq_ref, lat_ref, latb_ref, ropet_ref, ropetb_ref, *, n_q, n_rope):
    h = h_ref[...]
    r = _dot(h, w_ref[...])
    cq_ref[...] = _rms(r[:, :n_q], qn_ref[...]).astype(cq_ref.dtype)
    lat = _rms(r[:, n_q:], kvn_ref[...])
    lat_ref[...] = lat
    latb_ref[...] = lat.astype(latb_ref.dtype)
    rt = _dot_nt(wrt_ref[...], h)
    ropet = rt[:n_rope] * cost_ref[...] + rt[n_rope:] * sint_ref[...]
    ropet_ref[0] = ropet
    ropetb_ref[0] = ropet.astype(ropetb_ref.dtype)


def _mla_down_call(h, w, w_rope_t, q_norm, kv_norm, cos_t, sin_t, groups):
    m, d = h.shape
    n_q = q_norm.shape[0]
    n_lat = kv_norm.shape[0]
    n_rope, tpg = cos_t.shape
    tm = _row_tile(tpg)
    tiles = tpg // tm
    row = lambda n: pl.BlockSpec((tm, n), lambda i: (i, 0))
    const = lambda a: pl.BlockSpec(a.shape, lambda i: (0, 0))
    pos = pl.BlockSpec((n_rope, tm), lambda i: (0, i % tiles))
    rope_out = pl.BlockSpec((1, n_rope, tm), lambda i: (i // tiles, 0, i % tiles))
    qn = q_norm.reshape(1, n_q)
    kvn = kv_norm.reshape(1, n_lat)
    return pl.pallas_call(
        functools.partial(_mla_down_kernel, n_q=n_q, n_rope=n_rope),
        grid=(m // tm,),
        in_specs=[row(d), const(w), const(w_rope_t), const(qn), const(kvn), pos, pos],
        out_specs=[row(n_q), row(n_lat), row(n_lat), rope_out, rope_out],
        out_shape=[jax.ShapeDtypeStruct((m, n_q), h.dtype), jax.ShapeDtypeStruct((m, n_lat), F32),
                   jax.ShapeDtypeStruct((m, n_lat), h.dtype), jax.ShapeDtypeStruct((groups, n_rope, tpg), F32),
                   jax.ShapeDtypeStruct((groups, n_rope, tpg), h.dtype)],
        compiler_params=_params("parallel"),
        name="mla_down",
    )(h, w, w_rope_t, qn, kvn, cos_t, sin_t)


def _mla_uq_kernel(c_ref, w_ref, cos_ref, sin_ref, *out_refs, n_nope, n_rope, scale, head_major):
    c = c_ref[...]
    qn = (_dot(c, w_ref[:, :n_nope]) * scale).astype(c.dtype)
    reps = n_rope // cos_ref.shape[1]
    cos = jnp.concatenate([cos_ref[...]] * reps, axis=1)
    sin = jnp.concatenate([sin_ref[...]] * reps, axis=1)
    r = _dot(c, w_ref[:, n_nope:n_nope + n_rope])
    rot = _dot(c, w_ref[:, n_nope + n_rope:])
    qr = ((r * cos + rot * sin) * scale).astype(c.dtype)
    if head_major:
        (qc_ref,) = out_refs
        for hh in range(MLA_HEADS):
            qc_ref[hh] = jnp.concatenate(
                [qn[:, hh * MLA_NOPE:(hh + 1) * MLA_NOPE], qr[:, hh * MLA_ROPE:(hh + 1) * MLA_ROPE]], axis=1)
    else:
        qn_ref, qr_ref = out_refs
        qn_ref[...] = qn
        qr_ref[...] = qr


def _mla_uq_call(c_q, w, cos, sin, scale, head_major):
    m, k = c_q.shape
    n_nope = MLA_HEADS * MLA_NOPE
    n_rope = MLA_HEADS * MLA_ROPE
    tm = min(_row_tile(m), 256)
    n_pos_tiles = cos.shape[0] // tm
    pos = pl.BlockSpec((tm, cos.shape[1]), lambda i: (i % n_pos_tiles, 0))
    if head_major:
        width = MLA_NOPE + MLA_ROPE
        out_specs = [pl.BlockSpec((MLA_HEADS, tm, width), lambda i: (0, i, 0))]
        out_shape = [jax.ShapeDtypeStruct((MLA_HEADS, m, width), c_q.dtype)]
    else:
        out_specs = [pl.BlockSpec((tm, n_nope), lambda i: (i, 0)), pl.BlockSpec((tm, n_rope), lambda i: (i, 0))]
        out_shape = [jax.ShapeDtypeStruct((m, n_nope), c_q.dtype), jax.ShapeDtypeStruct((m, n_rope), c_q.dtype)]
    return pl.pallas_call(
        functools.partial(_mla_uq_kernel, n_nope=n_nope, n_rope=n_rope, scale=scale, head_major=head_major),
        grid=(m // tm,),
        in_specs=[pl.BlockSpec((tm, k), lambda i: (i, 0)), pl.BlockSpec(w.shape, lambda i: (0, 0)), pos, pos],
        out_specs=out_specs,
        out_shape=out_shape,
        compiler_params=_params("parallel"),
        name="mla_uq",
    )(c_q, w, cos, sin)


def _kv_expand_kernel(lat_ref, wkt_ref, wv_ref, knt_ref, v_ref):
    lat = lat_ref[...]
    knt_ref[0] = _dot_nt(wkt_ref[...], lat).astype(BF16)
    v_ref[...] = _dot(lat, wv_ref[...]).astype(BF16)


def _kv_expand_call(lat_b, w_uk_t, w_uv, batch, seq):
    m, c = lat_b.shape
    tm = _row_tile(seq)
    tiles = seq // tm
    return pl.pallas_call(
        _kv_expand_kernel,
        grid=(m // tm,),
        in_specs=[pl.BlockSpec((tm, c), lambda i: (i, 0)), pl.BlockSpec(w_uk_t.shape, lambda i: (0, 0)),
                  pl.BlockSpec(w_uv.shape, lambda i: (0, 0))],
        out_specs=[pl.BlockSpec((1, w_uk_t.shape[0], tm), lambda i: (i // tiles, 0, i % tiles)),
                   pl.BlockSpec((tm, w_uv.shape[1]), lambda i: (i, 0))],
        out_shape=[jax.ShapeDtypeStruct((batch, w_uk_t.shape[0], seq), BF16),
                   jax.ShapeDtypeStruct((m, w_uv.shape[1]), BF16)],
        compiler_params=_params("parallel"),
        name="mla_kv_expand",
    )(lat_b, w_uk_t, w_uv)


def _head_mm_kernel(x_ref, w_ref, o_ref, *, scale):
    o_ref[...] = (_dot(x_ref[...], w_ref[0]) * scale).astype(o_ref.dtype)


def _head_mm_call(x, w, out_dtype, scale=1.0):
    m = x.shape[0]
    h, a, b = w.shape
    return pl.pallas_call(
        functools.partial(_head_mm_kernel, scale=scale),
        grid=(h,),
        in_specs=[pl.BlockSpec((m, a), lambda i: (0, i)), pl.BlockSpec((1, a, b), lambda i: (i, 0, 0))],
        out_specs=pl.BlockSpec((m, b), lambda i: (0, i)),
        out_shape=jax.ShapeDtypeStruct((m, h * b), out_dtype),
        compiler_params=_params("parallel"),
        name="head_matmul",
    )(x, w)


def _mla_prompt_kernel(qc_ref, knt_ref, krt_ref, v_ref, o_ref, m_ref, l_ref, acc_ref, *, tq, hps):
    i = pl.program_id(2)
    m_ref[...] = jnp.full_like(m_ref, -jnp.inf)
    l_ref[...] = jnp.zeros_like(l_ref)
    acc_ref[...] = jnp.zeros_like(acc_ref)
    reps = tq // LANES

    def block(j, masked):
        start = pl.multiple_of(j * tq, tq)
        krt = krt_ref[0, :, pl.ds(start, tq)]
        for hh in range(hps):
            kt = jnp.concatenate([knt_ref[0, hh * MLA_NOPE:(hh + 1) * MLA_NOPE, pl.ds(start, tq)], krt], axis=0)
            s = _dot(qc_ref[hh], kt)
            if masked:
                r = lax.broadcasted_iota(jnp.int32, s.shape, 0)
                c = lax.broadcasted_iota(jnp.int32, s.shape, 1)
                s = jnp.where(c <= r, s, -jnp.inf)
            m_old = m_ref[hh]
            m_new = jnp.maximum(m_old, jnp.max(s, axis=1, keepdims=True))
            alpha = jnp.exp2(m_old - m_new)
            p = jnp.exp2(s - jnp.concatenate([m_new] * reps, axis=1))
            l_ref[hh] = alpha * l_ref[hh] + jnp.sum(p, axis=1, keepdims=True)
            pv = _dot(p.astype(BF16), v_ref[pl.ds(start, tq), hh * MLA_V:(hh + 1) * MLA_V])
            acc_ref[hh] = alpha * acc_ref[hh] + pv
            m_ref[hh] = m_new

    def body(j, carry):
        block(j, False)
        return carry

    lax.fori_loop(0, i, body, 0)
    block(i, True)
    for hh in range(hps):
        o_ref[:, hh * MLA_V:(hh + 1) * MLA_V] = (acc_ref[hh] / l_ref[hh]).astype(o_ref.dtype)


def _mla_prompt_call(qc, knt, krt, v, batch, seq, tq, hps=2):
    n_qt = seq // tq
    width = MLA_NOPE + MLA_ROPE
    return pl.pallas_call(
        functools.partial(_mla_prompt_kernel, tq=tq, hps=hps),
        grid=(batch, MLA_HEADS // hps, n_qt),
        in_specs=[pl.BlockSpec((hps, tq, width), lambda b, h, i: (h, b * n_qt + i, 0)),
                  pl.BlockSpec((1, hps * MLA_NOPE, seq), lambda b, h, i: (b, h, 0)),
                  pl.BlockSpec((1, MLA_ROPE, seq), lambda b, h, i: (b, 0, 0)),
                  pl.BlockSpec((seq, hps * MLA_V), lambda b, h, i: (b, h))],
        out_specs=pl.BlockSpec((tq, hps * MLA_V), lambda b, h, i: (b * n_qt + i, h)),
        out_shape=jax.ShapeDtypeStruct((batch * seq, MLA_HEADS * MLA_V), BF16),
        scratch_shapes=[pltpu.VMEM((hps, tq, LANES), F32), pltpu.VMEM((hps, tq, LANES), F32),
                        pltpu.VMEM((hps, tq, MLA_V), F32)],
        compiler_params=_params("parallel", "parallel", "arbitrary"),
        name="mla_attn_prompt",
    )(qc, knt, krt, v)


def _mla_decode_kernel(pt_ref, ql_ref, qr_ref, ln_ref, rn_ref, *refs, pages):
    lat_refs = refs[:pages]
    kr_refs = refs[pages:2 * pages]
    o_ref, m_ref, l_ref, acc_ref = refs[2 * pages:]
    s_idx = pl.program_id(1)
    ql = ql_ref[0]
    qr = qr_ref[0]
    cd = ql.dtype

    @pl.when(s_idx == 0)
    def _():
        ln = ln_ref[0].astype(F32)
        rn = rn_ref[0].astype(F32)
        s_new = (jnp.sum(ql.astype(F32) * ln, axis=1, keepdims=True)
                 + jnp.sum(qr.astype(F32) * rn, axis=1, keepdims=True)) * MLA_SCALE
        m_ref[...] = s_new
        l_ref[...] = jnp.ones_like(l_ref)
        acc_ref[...] = jnp.broadcast_to(ln, acc_ref.shape)

    lat = jnp.concatenate([r[0] for r in lat_refs], axis=0).astype(cd)
    krt = jnp.concatenate([r[0] for r in kr_refs], axis=1).astype(cd)
    s = (_dot_nt(ql, lat) + _dot(qr, krt)) * MLA_SCALE
    m_old = m_ref[...]
    m_new = jnp.maximum(m_old, jnp.max(s, axis=1, keepdims=True))
    alpha = jnp.exp(m_old - m_new)
    p = jnp.exp(s - m_new)
    l_ref[...] = alpha * l_ref[...] + jnp.sum(p, axis=1, keepdims=True)
    acc_ref[...] = alpha * acc_ref[...] + _dot(p.astype(cd), lat)
    m_ref[...] = m_new

    @pl.when(s_idx == pl.num_programs(1) - 1)
    def _():
        o_ref[0] = (acc_ref[...] / l_ref[...]).astype(o_ref.dtype)


def _mla_decode_call(q_lat, q_rope, lat_new, rope_new, cache_lat, cache_rope, page_table, pages=16):
    b, heads, lora = q_lat.shape
    n_rope = q_rope.shape[2]
    n_pages = page_table.shape[1]
    page = cache_lat.shape[1]
    steps = n_pages // pages

    def page_spec(t, shape):
        return pl.BlockSpec((1,) + shape, lambda bi, s, pt: (pt[bi, s * pages + t], 0, 0))

    per_b = lambda shape: pl.BlockSpec((1,) + shape, lambda bi, s, pt: (bi, 0, 0))
    grid_spec = pltpu.PrefetchScalarGridSpec(
        num_scalar_prefetch=1,
        grid=(b, steps),
        in_specs=[per_b((heads, lora)), per_b((heads, n_rope)), per_b((1, lora)), per_b((1, n_rope))]
        + [page_spec(t, (page, lora)) for t in range(pages)] + [page_spec(t, (n_rope, page)) for t in range(pages)],
        out_specs=per_b((heads, lora)),
        scratch_shapes=[pltpu.VMEM((heads, 1), F32), pltpu.VMEM((heads, 1), F32), pltpu.VMEM((heads, lora), F32)],
    )
    return pl.pallas_call(
        functools.partial(_mla_decode_kernel, pages=pages),
        grid_spec=grid_spec,
        out_shape=jax.ShapeDtypeStruct((b, heads, lora), q_lat.dtype),
        compiler_params=_params("parallel", "arbitrary"),
        name="mla_attn_decode",
    )(page_table, q_lat, q_rope, lat_new.reshape(b, 1, lora), rope_new.reshape(b, 1, n_rope),
      *([cache_lat] * pages), *([cache_rope] * pages))


def _ffn_kernel(h_ref, g_ref, wg_ref, wu_ref, wd_ref, x_ref, o_ref, y_ref, out_ref, *, n_e, n_f):
    e = pl.program_id(1)
    f = pl.program_id(2)

    @pl.when(jnp.logical_and(e == 0, f == 0))
    def _():
        out_ref[...] = jnp.zeros_like(out_ref)

    @pl.when(f == 0)
    def _():
        y_ref[...] = jnp.zeros_like(y_ref)

    h = h_ref[...]
    g = _dot(h, wg_ref[0])
    u = _dot(h, wu_ref[0])
    act = (g * jax.nn.sigmoid(g)) * u
    y_ref[...] += _dot(act.astype(h.dtype), wd_ref[0])

    @pl.when(f == n_f - 1)
    def _():
        gates = g_ref[...]
        lane = lax.broadcasted_iota(jnp.int32, gates.shape, 1)
        gate = jnp.sum(jnp.where(lane == e, gates, 0.0), axis=1, keepdims=True)
        out_ref[...] += gate * y_ref[...]

    @pl.when(jnp.logical_and(e == n_e - 1, f == n_f - 1))
    def _():
        o_ref[...] = x_ref[...] + out_ref[...]


def _ffn_tile(wg):
    _, d, ff = wg.shape
    fits = [tf for tf in range(LANES, ff + 1, LANES)
            if ff % tf == 0 and d * tf * wg.dtype.itemsize <= FFN_BLOCK_BYTES]
    return max(fits) if fits else ff


def _ffn_call(h, gates, wg, wu, wd, x, tf):
    m, d = h.shape
    n_e, _, ff = wg.shape
    n_f = ff // tf
    tm = _row_tile(m)
    return pl.pallas_call(
        functools.partial(_ffn_kernel, n_e=n_e, n_f=n_f),
        grid=(m // tm, n_e, n_f),
        in_specs=[pl.BlockSpec((tm, d), lambda i, e, f: (i, 0)),
                  pl.BlockSpec((tm, LANES), lambda i, e, f: (i, 0)),
                  pl.BlockSpec((1, d, tf), lambda i, e, f: (e, 0, f)),
                  pl.BlockSpec((1, d, tf), lambda i, e, f: (e, 0, f)),
                  pl.BlockSpec((1, tf, d), lambda i, e, f: (e, f, 0)),
                  pl.BlockSpec((tm, d), lambda i, e, f: (i, 0))],
        out_specs=pl.BlockSpec((tm, d), lambda i, e, f: (i, 0)),
        out_shape=jax.ShapeDtypeStruct((m, d), F32),
        scratch_shapes=[pltpu.VMEM((tm, d), F32), pltpu.VMEM((tm, d), F32)],
        compiler_params=_params("parallel", "arbitrary", "arbitrary"),
        name="ffn",
    )(h, gates, wg, wu, wd, x)


def _router_kernel(h_ref, w_ref, g_ref, id_ref, pw_ref):
    logits = _dot(h_ref[...], w_ref[...])
    lane = lax.broadcasted_iota(jnp.int32, logits.shape, 1)
    logits = jnp.where(lane < N_EXPERTS, logits, -jnp.inf)
    v1 = jnp.max(logits, axis=1, keepdims=True)
    i1 = jnp.min(jnp.where(logits == v1, lane, LANES), axis=1, keepdims=True)
    rest = jnp.where(lane == i1, -jnp.inf, logits)
    v2 = jnp.max(rest, axis=1, keepdims=True)
    i2 = jnp.min(jnp.where(rest == v2, lane, LANES), axis=1, keepdims=True)
    e2 = jnp.exp(v2 - v1)
    w1 = 1.0 / (1.0 + e2)
    w2 = e2 / (1.0 + e2)
    g_ref[...] = jnp.where(lane == i1, w1, 0.0) + jnp.where(lane == i2, w2, 0.0)
    first = i1 < i2
    lo, hi = jnp.where(first, i1, i2), jnp.where(first, i2, i1)
    w_lo, w_hi = jnp.where(first, w1, w2), jnp.where(first, w2, w1)
    id_ref[...] = jnp.where(lane == 0, lo, jnp.where(lane == 1, hi, 0))
    pw_ref[...] = jnp.where(lane == 0, w_lo, jnp.where(lane == 1, w_hi, 0.0))


def _router_call(h, w_pad):
    m, d = h.shape
    tm = _row_tile(m)
    out = pl.BlockSpec((tm, LANES), lambda i: (i, 0))
    return pl.pallas_call(
        _router_kernel,
        grid=(m // tm,),
        in_specs=[pl.BlockSpec((tm, d), lambda i: (i, 0)), pl.BlockSpec((d, LANES), lambda i: (0, 0))],
        out_specs=[out, out, out],
        out_shape=[jax.ShapeDtypeStruct((m, LANES), F32), jax.ShapeDtypeStruct((m, LANES), jnp.int32),
                   jax.ShapeDtypeStruct((m, LANES), F32)],
        compiler_params=_params("parallel"),
        name="router",
    )(h, w_pad)


def _route(pair_ids, tm):
    m = pair_ids.shape[0]
    e_flat = pair_ids.reshape(-1)
    onehot = (e_flat[:, None] == jnp.arange(N_EXPERTS, dtype=jnp.int32)[None, :]).astype(jnp.int32)
    csum = jnp.cumsum(onehot, axis=0)
    rank = jnp.sum(csum * onehot, axis=1) - 1
    counts = csum[-1]
    tiles_e = (counts + tm - 1) // tm
    tile_end = jnp.cumsum(tiles_e)
    seg_start = (tile_end - tiles_e) * tm
    pos = jnp.sum(seg_start[None, :] * onehot, axis=1) + rank
    n_tiles = _moe_tiles(m, tm)
    tile_expert = jnp.sum(jnp.arange(n_tiles, dtype=jnp.int32)[:, None] >= tile_end[None, :], axis=1)
    tile_expert = jnp.minimum(tile_expert, N_EXPERTS - 1).astype(jnp.int32)
    return pos.reshape(m, 2).astype(jnp.int32), tile_expert, tile_end[-1:].astype(jnp.int32)


def _moe_tiles(m, tm):
    return (2 * m) // tm + N_EXPERTS


def _moe_dispatch_kernel(pos_ref, x_ref, xg_in_ref, xg_ref, sem):
    del xg_in_ref
    tm = x_ref.shape[0]

    def body(r, carry):
        for k in range(2):
            dst = pos_ref[0, 0, 2 * r + k]
            pltpu.make_async_copy(x_ref.at[pl.ds(r, 1)], xg_ref.at[pl.ds(dst, 1)], sem).start()
        return carry

    lax.fori_loop(0, tm, body, 0, unroll=8)
    for _ in range(2):
        pltpu.make_async_copy(x_ref, xg_ref.at[pl.ds(0, tm)], sem).wait()


def _moe_dispatch_call(x, pos, n_rows):
    m, d = x.shape
    tm = _row_tile(m)
    pos3 = pos.reshape(m // tm, 1, 2 * tm)
    return pl.pallas_call(
        _moe_dispatch_kernel,
        grid=(m // tm,),
        in_specs=[pl.BlockSpec((1, 1, 2 * tm), lambda i: (i, 0, 0), memory_space=pltpu.SMEM),
                  pl.BlockSpec((tm, d), lambda i: (i, 0)),
                  pl.BlockSpec(memory_space=pl.ANY)],
        out_specs=pl.BlockSpec(memory_space=pl.ANY),
        out_shape=jax.ShapeDtypeStruct((n_rows, d), x.dtype),
        scratch_shapes=[pltpu.SemaphoreType.DMA(())],
        input_output_aliases={2: 0},
        compiler_params=_params("arbitrary"),
        name="moe_dispatch",
    )(pos3, x, jnp.zeros((n_rows, d), x.dtype))


def _moe_ffn_kernel(te_ref, nu_ref, x_ref, g_ref, wg_ref, wu_ref, wd_ref, y_ref, h_ref, acc_ref, *, n_f):
    t = pl.program_id(0)
    f = pl.program_id(1)
    used = t < nu_ref[0]

    @pl.when(jnp.logical_and(used, f == 0))
    def _():
        h_ref[...] = _rms(x_ref[...], g_ref[...]).astype(BF16)
        acc_ref[...] = jnp.zeros_like(acc_ref)

    @pl.when(used)
    def _():
        h = h_ref[...]
        g = _dot(h, wg_ref[0])
        u = _dot(h, wu_ref[0])
        act = (g * jax.nn.sigmoid(g)) * u
        acc_ref[...] += _dot(act.astype(BF16), wd_ref[0])

    @pl.when(jnp.logical_and(used, f == n_f - 1))
    def _():
        y_ref[...] = acc_ref[...]

    @pl.when(jnp.logical_and(jnp.logical_not(used), f == n_f - 1))
    def _():
        y_ref[...] = jnp.zeros_like(y_ref)


def _moe_ffn_call(xg, g, tile_expert, n_used, wg, wu, wd, tm, tf):
    n_rows, d = xg.shape
    ff = wg.shape[2]
    n_f = ff // tf
    grid_spec = pltpu.PrefetchScalarGridSpec(
        num_scalar_prefetch=2,
        grid=(n_rows // tm, n_f),
        in_specs=[pl.BlockSpec((tm, d), lambda t, f, te, nu: (t, 0)),
                  pl.BlockSpec((1, d), lambda t, f, te, nu: (0, 0)),
                  pl.BlockSpec((1, d, tf), lambda t, f, te, nu: (te[t], 0, f)),
                  pl.BlockSpec((1, d, tf), lambda t, f, te, nu: (te[t], 0, f)),
                  pl.BlockSpec((1, tf, d), lambda t, f, te, nu: (te[t], f, 0))],
        out_specs=pl.BlockSpec((tm, d), lambda t, f, te, nu: (t, 0)),
        scratch_shapes=[pltpu.VMEM((tm, d), BF16), pltpu.VMEM((tm, d), F32)],
    )
    return pl.pallas_call(
        functools.partial(_moe_ffn_kernel, n_f=n_f),
        grid_spec=grid_spec,
        out_shape=jax.ShapeDtypeStruct((n_rows, d), F32),
        compiler_params=_params("parallel", "arbitrary"),
        name="moe_ffn",
    )(tile_expert, n_used, xg, g.reshape(1, d), wg, wu, wd)


def _moe_combine_kernel(pos_ref, x_ref, w_ref, g_ref, y_ref, o_ref, ya_ref, yb_ref, sem):
    tm = x_ref.shape[0]

    def body(r, carry):
        pltpu.make_async_copy(y_ref.at[pl.ds(pos_ref[0, 0, 2 * r], 1)], ya_ref.at[pl.ds(r, 1)], sem).start()
        pltpu.make_async_copy(y_ref.at[pl.ds(pos_ref[0, 0, 2 * r + 1], 1)], yb_ref.at[pl.ds(r, 1)], sem).start()
        return carry

    lax.fori_loop(0, tm, body, 0, unroll=8)
    pltpu.make_async_copy(y_ref.at[pl.ds(0, tm)], ya_ref, sem).wait()
    pltpu.make_async_copy(y_ref.at[pl.ds(0, tm)], yb_ref, sem).wait()
    w = w_ref[...]
    x = x_ref[...] + (w[:, 0:1] * ya_ref[...] + w[:, 1:2] * yb_ref[...])
    o_ref[...] = _rms(x, g_ref[...])


def _moe_combine_call(x, pair_w, pos, y, g):
    m, d = x.shape
    tm = _row_tile(m)
    pos3 = pos.reshape(m // tm, 1, 2 * tm)
    return pl.pallas_call(
        _moe_combine_kernel,
        grid=(m // tm,),
        in_specs=[pl.BlockSpec((1, 1, 2 * tm), lambda i: (i, 0, 0), memory_space=pltpu.SMEM),
                  pl.BlockSpec((tm, d), lambda i: (i, 0)),
                  pl.BlockSpec((tm, LANES), lambda i: (i, 0)),
                  pl.BlockSpec((1, d), lambda i: (0, 0)),
                  pl.BlockSpec(memory_space=pl.ANY)],
        out_specs=pl.BlockSpec((tm, d), lambda i: (i, 0)),
        out_shape=jax.ShapeDtypeStruct((m, d), F32),
        scratch_shapes=[pltpu.VMEM((tm, d), F32), pltpu.VMEM((tm, d), F32), pltpu.SemaphoreType.DMA(())],
        compiler_params=_params("arbitrary"),
        name="moe_combine",
    )(pos3, x, pair_w, g.reshape(1, d), y)


def _rope_tables(pos, width):
    half = MLA_ROPE // 2
    freqs = ROPE_THETA ** (-jnp.arange(half, dtype=F32) * 2.0 / MLA_ROPE)
    ang = pos.astype(F32)[:, None] * freqs[None, :]
    reps = width // half
    return jnp.tile(jnp.cos(ang), (1, reps)), jnp.tile(jnp.sin(ang), (1, reps))


def _rope_tables_t(pos):
    cos, sin = _rope_tables(pos, MLA_ROPE)
    return cos.T, sin.T


def _rotate_half_columns(w):
    k, n = w.shape
    half = MLA_ROPE // 2
    w4 = w.reshape(k, n // MLA_ROPE, 2, half)
    return jnp.concatenate([-w4[:, :, 1:2], w4[:, :, 0:1]], axis=2).reshape(k, n)


def _prepare_weights(sb_w_qkv, sb_w_o, mla_w_dq, mla_w_uq, mla_w_dkv, mla_w_uk, mla_w_uv, mla_w_o,
                     ffn_w_gate, ffn_w_up, ffn_w_down, moe_w_router, moe_w_gate, moe_w_up, moe_w_down, dtype):
    kv_lora = mla_w_uk.shape[0]
    w_rope_k = mla_w_dkv[:, kv_lora:]
    w_down = jnp.concatenate([mla_w_dq, mla_w_dkv[:, :kv_lora]], axis=1)
    w_rope_t = jnp.concatenate([w_rope_k, _rotate_half_columns(w_rope_k)], axis=1).T
    q_lora = mla_w_uq.shape[0]
    uq = mla_w_uq.reshape(q_lora, MLA_HEADS, MLA_NOPE + MLA_ROPE)
    uq_nope = uq[:, :, :MLA_NOPE].reshape(q_lora, MLA_HEADS * MLA_NOPE)
    uq_rope = uq[:, :, MLA_NOPE:].reshape(q_lora, MLA_HEADS * MLA_ROPE)
    w_uq = jnp.concatenate([uq_nope, uq_rope, _rotate_half_columns(uq_rope)], axis=1)
    nq = SB_HEADS * SB_HEAD_DIM
    nkv = SB_KV_HEADS * SB_HEAD_DIM
    router = jnp.pad(moe_w_router, ((0, 0), (0, LANES - N_EXPERTS)))
    bf = lambda a: a.astype(dtype)
    return dict(
        sb_q=bf(sb_w_qkv[:, :nq]), sb_v=bf(sb_w_qkv[:, nq + nkv:]), sb_kv_t=bf(sb_w_qkv[:, nq:].T),
        sb_o=bf(sb_w_o), down=bf(w_down), rope_t=bf(w_rope_t), uq=bf(w_uq),
        uk_t=bf(mla_w_uk.reshape(kv_lora, -1).T), uv=bf(mla_w_uv.reshape(kv_lora, -1)),
        uk_heads=bf(jnp.transpose(mla_w_uk, (1, 2, 0))),
        uv_heads=bf(jnp.transpose(mla_w_uv, (1, 0, 2))),
        mla_o=bf(mla_w_o), ffn_gate=bf(ffn_w_gate)[None], ffn_up=bf(ffn_w_up)[None], ffn_down=bf(ffn_w_down)[None],
        router=bf(router), moe_gate=bf(moe_w_gate), moe_up=bf(moe_w_up), moe_down=bf(moe_w_down))


def _trunk(x, pos, w, norms, past):
    ln_mix, ln_ffn, ln_final, mla_q_norm, mla_kv_norm = norms
    batch, seq, d = x.shape
    m = batch * seq
    x2 = x.reshape(m, d)
    prompt = past is None
    groups = batch if prompt else 1
    tpg = m // groups

    cd = w["sb_q"].dtype
    h = _norm_call(x2, ln_mix[0], cd)
    q, kt_new, vt_new, ktb, vb = _qkv_call(h, w["sb_q"], w["sb_v"], w["sb_kv_t"], groups)
    if prompt:
        o = _sb_prompt_call(q, ktb, vb, batch, seq)
    else:
        cache_k, cache_v, cache_lat, cache_rope, page_table = past
        width = SB_KV_HEADS * SB_HEAD_DIM
        q3 = q.reshape(m, SB_HEADS, SB_HEAD_DIM)
        own = (jnp.arange(SB_HEADS)[:, None] // SB_GROUP) == (jnp.arange(width)[None, :] // SB_HEAD_DIM)
        q_bd = jnp.where(own[None], jnp.tile(q3, (1, 1, SB_KV_HEADS)), jnp.zeros((), cd))
        pages_t = lambda c: jnp.transpose(c, (0, 2, 3, 1)).reshape(c.shape[0], width, c.shape[1])
        ck, cv = pages_t(cache_k), pages_t(cache_v)
        n_pages = page_table.shape[1]
        pps = SB_DECODE_PAGES
        split = n_pages - pps
        o_tail, acc, c = _sb_decode_call(q_bd, ck, cv, page_table, split, n_pages, jnp.zeros((m, SB_HEADS, width), F32),
                                         jnp.zeros((m, SB_HEADS, LANES), F32), pps)
        o = lax.cond(jnp.max(c) > EXP_UNDERFLOW,
                     lambda: _sb_decode_call(q_bd, ck, cv, page_table, 0, split, acc, c, pps)[0],
                     lambda: o_tail)
        o = o.reshape(m, SB_HEADS * SB_HEAD_DIM)
    x2, h = _proj_res_norm_call(o, w["sb_o"], x2, ln_ffn[0])

    ones = jnp.ones((m, LANES), F32)
    x2 = _ffn_call(h, ones, w["ffn_gate"], w["ffn_up"], w["ffn_down"], x2, tf=_ffn_tile(w["ffn_gate"]))

    h = _norm_call(x2, ln_mix[1], cd)
    pos_rows = pos if prompt else jnp.tile(pos, batch)
    cos, sin = _rope_tables(pos_rows, LANES)
    cos_t, sin_t = _rope_tables_t(pos_rows)
    n_lat = mla_kv_norm.shape[0]
    c_q, lat_new, lat_b, ropet_new, ropet_b = _mla_down_call(
        h, w["down"], w["rope_t"], mla_q_norm, mla_kv_norm, cos_t, sin_t, groups)
    if prompt:
        (qc,) = _mla_uq_call(c_q, w["uq"], cos, sin, MLA_SCALE * LOG2_E, head_major=True)
        knt, v = _kv_expand_call(lat_b, w["uk_t"], w["uv"], batch, seq)
        o = _mla_prompt_call(qc, knt, ropet_b, v, batch, seq, tq=min(MLA_TQ, seq))
    else:
        qn, qr = _mla_uq_call(c_q, w["uq"], cos, sin, 1.0, head_major=False)
        q_lat = _head_mm_call(qn, w["uk_heads"], cd).reshape(m, MLA_HEADS, n_lat)
        bf = lambda a: a.astype(BF16)
        o_lat = _mla_decode_call(bf(q_lat), bf(qr.reshape(m, MLA_HEADS, MLA_ROPE)), bf(lat_b), bf(ropet_b[0].T),
                                 cache_lat, jnp.transpose(cache_rope, (0, 2, 1)), page_table)
        o = _head_mm_call(o_lat.reshape(m, MLA_HEADS * n_lat).astype(cd), w["uv_heads"], cd)
    x2, h = _proj_res_norm_call(o, w["mla_o"], x2, ln_ffn[1])

    gates, pair_ids, pair_w = _router_call(h, w["router"])
    if prompt:
        pos, tile_expert, n_used = _route(pair_ids[:, :2], MOE_TM)
        xg = _moe_dispatch_call(x2, pos, _moe_tiles(m, MOE_TM) * MOE_TM)
        yg = _moe_ffn_call(xg, ln_ffn[1], tile_expert, n_used, w["moe_gate"], w["moe_up"], w["moe_down"],
                           MOE_TM, MOE_TF)
        y = _moe_combine_call(x2, pair_w, pos, yg, ln_final)
    else:
        x2 = _ffn_call(h, gates, w["moe_gate"], w["moe_up"], w["moe_down"], x2, tf=_ffn_tile(w["moe_gate"]))
        y = _norm_call(x2, ln_final, F32)
    def kv_rows(t):
        t = jnp.transpose(t.reshape(groups, SB_KV_HEADS, SB_HEAD_DIM, tpg), (0, 3, 1, 2))
        return t.reshape(batch, seq, SB_KV_HEADS, SB_HEAD_DIM)

    rope_new = jnp.transpose(ropet_new, (0, 2, 1)).reshape(batch, seq, MLA_ROPE)
    return (y.reshape(batch, seq, d), kv_rows(kt_new), kv_rows(vt_new),
            lat_new.reshape(batch, seq, n_lat), rope_new)


def kernel(x_prompt, x_sample, cache_sb_k, cache_sb_v, cache_mla_latent, cache_mla_krope, page_table,
           ln_mix, ln_ffn, ln_final, sb_w_qkv, sb_w_o, mla_w_dq, mla_q_norm, mla_w_uq, mla_w_dkv,
           mla_kv_norm, mla_w_uk, mla_w_uv, mla_w_o, ffn_w_gate, ffn_w_up, ffn_w_down,
           moe_w_router, moe_w_gate, moe_w_up, moe_w_down):
    raw = (sb_w_qkv, sb_w_o, mla_w_dq, mla_w_uq, mla_w_dkv, mla_w_uk, mla_w_uv, mla_w_o,
           ffn_w_gate, ffn_w_up, ffn_w_down, moe_w_router, moe_w_gate, moe_w_up, moe_w_down)
    norms = (ln_mix, ln_ffn, ln_final, mla_q_norm, mla_kv_norm)

    pos_prompt = jnp.arange(x_prompt.shape[1], dtype=jnp.int32)
    y_p, k_p, v_p, lat_p, rope_p = _trunk(x_prompt, pos_prompt, _prepare_weights(*raw, BF16), norms, None)

    past_len = page_table.shape[1] * cache_sb_k.shape[1]
    pos_sample = past_len + jnp.arange(x_sample.shape[1], dtype=jnp.int32)
    past = (cache_sb_k, cache_sb_v, cache_mla_latent, cache_mla_krope, page_table)
    y_s, k_s, v_s, lat_s, rope_s = _trunk(x_sample, pos_sample, _prepare_weights(*raw, BF16), norms, past)

    return (y_p, y_s, k_p, v_p, lat_p, rope_p, k_s, v_s, lat_s, rope_s)
```
